```python
import math
import jax, jax.numpy as jnp
from jax import lax
import numpy as np

D_MODEL = 2048
BATCH = 2
SEQ = 4096
DEPTH = 4
DEC_BATCH = 8
DEC_SEQ = 1
PAST_LEN = 16384
PAGE_SIZE = 128

CONV_WIDTH_A = 31
W_A = D_MODEL // 2
N_GROUPS_A = 8
SB_HEAD_DIM = 128
SB_HEADS = (D_MODEL // 2) // SB_HEAD_DIM
W_B = SB_HEADS * SB_HEAD_DIM
SB_BLOCK = 128
SB_BIAS_INIT = -7.0
IN_EVEN = 3 * W_A + 4 * W_B
GDN_HEAD_DIM = 128
GDN_QK_HEADS = D_MODEL // GDN_HEAD_DIM
GDN_V_HEADS = 2 * GDN_QK_HEADS
GDN_QK_W = GDN_QK_HEADS * GDN_HEAD_DIM
GDN_V_W = GDN_V_HEADS * GDN_HEAD_DIM
GDN_CONV_WIDTH = 4
GDN_CONV_DIM = 2 * GDN_QK_W + GDN_V_W
GDN_CHUNK = 64
IN_ODD = GDN_CONV_DIM + GDN_V_W + 2 * GDN_V_HEADS
N_EVEN = (DEPTH + 1) // 2
N_ODD = DEPTH // 2
DEEPNORM_ALPHA = (2 * DEPTH) ** 0.25
DEEPNORM_BETA = (8 * DEPTH) ** -0.25
LN_EPS = 1e-5
RMS_EPS = 1e-6
F32 = jnp.float32

kernel_name = "hybrid_conv_stickbreak_gdn_decoder_step"


def _layer_norm(x, g, b):
    xf = x.astype(F32)
    mu = jnp.mean(xf, axis=-1, keepdims=True)
    var = jnp.mean(jnp.square(xf - mu), axis=-1, keepdims=True)
    y = (xf - mu) * lax.rsqrt(var + LN_EPS)
    return (y * g.astype(F32) + b.astype(F32)).astype(x.dtype)


def _group_norm_tokens(x, g, b):
    shp = x.shape
    xf = x.astype(F32).reshape(shp[:-1] + (N_GROUPS_A, shp[-1] // N_GROUPS_A))
    mu = jnp.mean(xf, axis=-1, keepdims=True)
    var = jnp.mean(jnp.square(xf - mu), axis=-1, keepdims=True)
    y = ((xf - mu) * lax.rsqrt(var + LN_EPS)).reshape(shp)
    return (y * g.astype(F32) + b.astype(F32)).astype(x.dtype)


def _l2_normalize(x):
    return x * lax.rsqrt(jnp.sum(jnp.square(x), axis=-1, keepdims=True) + RMS_EPS)


def _causal_depthwise_conv(x, buf, w):
    full = jnp.concatenate([buf.astype(x.dtype), x], axis=1)
    y = lax.conv_general_dilated(full, w.astype(x.dtype)[:, None, :], window_strides=(1,), padding='VALID',
                                 dimension_numbers=('NWC', 'WIO', 'NWC'), feature_group_count=x.shape[-1])
    return y, full[:, full.shape[1] - (w.shape[0] - 1):]


def _adaln(c, w_ada, b_ada):
    mod = jax.nn.silu(c) @ w_ada + b_ada
    shift, scale, gate = jnp.split(mod, 3, axis=-1)
    return shift[:, None, :], scale[:, None, :], gate[:, None, :]


def _stick_breaking_block(q, q_pos, k, v, k_pos, bias):
    z = jnp.einsum('bqhd,bkhd->bhqk', q.astype(F32), k.astype(F32)) * (SB_HEAD_DIM ** -0.5)
    z = z + bias.astype(F32)[None, :, None, None]
    visible = k_pos[None, :] < q_pos[:, None]
    neg_log_keep = jnp.where(visible, jax.nn.softplus(z), 0.0)
    later = lax.cumsum(neg_log_keep, axis=3, reverse=True) - neg_log_keep
    weights = jnp.where(visible, jnp.exp(jax.nn.log_sigmoid(z) - later), 0.0)
    return jnp.einsum('bhqk,bkhd->bqhd', weights, v.astype(F32))


def _stick_breaking_prompt(q, k, v, bias):
    b, t, h, d = q.shape
    nb = t // SB_BLOCK
    pos = jnp.arange(t)
    qb = jnp.moveaxis(q.reshape(b, nb, SB_BLOCK, h, d), 1, 0)
    pb = pos.reshape(nb, SB_BLOCK)
    ob = lax.map(lambda a: _stick_breaking_block(a[0], a[1], k, v, pos, bias), (qb, pb))
    return jnp.moveaxis(ob, 0, 1).reshape(b, t, h, d)


def _stick_breaking_with_past(k_past, v_past):
    def attend(q, k, v, bias):
        past = k_past.shape[1]
        kk = jnp.concatenate([k_past.astype(k.dtype), k], axis=1)
        vv = jnp.concatenate([v_past.astype(v.dtype), v], axis=1)
        q_pos = past + jnp.arange(q.shape[1])
        return _stick_breaking_block(q, q_pos, kk, vv, jnp.arange(kk.shape[1]), bias)
    return attend


def _even_mixer(u, conv_buf, attend, w_in, conv_w, gn_g, gn_b, sb_bias, w_out):
    b, t, _ = u.shape
    h = u @ w_in
    a_val, a_glu, a_gate, q, k, v, b_gate = jnp.split(
        h, [W_A, 2 * W_A, 3 * W_A, 3 * W_A + W_B, 3 * W_A + 2 * W_B, 3 * W_A + 3 * W_B], axis=-1)
    glu = a_val * jax.nn.sigmoid(a_glu)
    y_a, new_buf = _causal_depthwise_conv(glu, conv_buf, conv_w)
    y_a = jax.nn.silu(_group_norm_tokens(y_a, gn_g, gn_b)) * jax.nn.silu(a_gate)
    q = q.reshape(b, t, SB_HEADS, SB_HEAD_DIM)
    k = k.reshape(b, t, SB_HEADS, SB_HEAD_DIM)
    v = v.reshape(b, t, SB_HEADS, SB_HEAD_DIM)
    o_b = attend(q, k, v, sb_bias)
    y_b = o_b.reshape(b, t, W_B).astype(u.dtype) * jax.nn.silu(b_gate)
    out = jnp.concatenate([y_a, y_b], axis=-1) @ w_out
    return out, new_buf, k, v


def _gated_delta_chunked(q, k, v, g, beta, s0):
    b, t, h, _ = q.shape
    dv = v.shape[-1]
    c = GDN_CHUNK
    n = t // c

    def chunks(x):
        return jnp.moveaxis(x, 2, 1).reshape((b, h, n, c) + x.shape[3:])

    qc, kc, vc = chunks(q), chunks(k), chunks(v)
    bc = chunks(beta)
    gc = jnp.cumsum(chunks(g), axis=-1)
    idx = jnp.arange(c)
    incl = idx[:, None] >= idx[None, :]
    strict = idx[:, None] > idx[None, :]
    diff = gc[..., :, None] - gc[..., None, :]
    decay = jnp.where(incl, jnp.exp(jnp.where(incl, diff, 0.0)), 0.0)
    kb = kc * bc[..., None]
    lmat = jnp.where(strict, jnp.einsum('bhnid,bhnjd->bhnij', kb, kc) * decay, 0.0)
    eye = jnp.eye(c, dtype=F32)
    tmat = lax.linalg.triangular_solve(lmat + eye, jnp.broadcast_to(eye, lmat.shape),
                                       left_side=True, lower=True, unit_diagonal=True)
    u_val = tmat @ (vc * bc[..., None])
    w_key = tmat @ (kb * jnp.exp(gc)[..., None])
    intra = jnp.where(incl, jnp.einsum('bhnid,bhnjd->bhnij', qc, kc) * decay, 0.0)
    q_dec = qc * jnp.exp(gc)[..., None]
    k_dec = kc * jnp.exp(gc[..., -1:] - gc)[..., None]
    g_last = jnp.exp(gc[..., -1])

    def step(s, xs):
        u_n, w_n, qd_n, kd_n, a_n, gl_n = xs
        v_new = u_n - w_n @ s
        o_n = qd_n @ s + a_n @ v_new
        s = s * gl_n[..., None, None] + jnp.swapaxes(kd_n, -1, -2) @ v_new
        return s, o_n

    xs = tuple(jnp.moveaxis(x, 2, 0) for x in (u_val, w_key, q_dec, k_dec, intra, g_last))
    s_fin, o = lax.scan(step, s0, xs)
    o = jnp.moveaxis(o, 0, 2).reshape(b, h, t, dv)
    return jnp.moveaxis(o, 1, 2), s_fin


def _gated_delta_recurrent(q, k, v, g, beta, s0):
    def step(s, xs):
        q_t, k_t, v_t, g_t, b_t = xs
        s = s * jnp.exp(g_t)[..., None, None]
        delta = (v_t - jnp.einsum('bhkv,bhk->bhv', s, k_t)) * b_t[..., None]
        s = s + jnp.einsum('bhk,bhv->bhkv', k_t, delta)
        return s, jnp.einsum('bhkv,bhk->bhv', s, q_t)

    xs = tuple(jnp.moveaxis(x, 1, 0) for x in (q, k, v, g, beta))
    s_fin, o = lax.scan(step, s0, xs)
    return jnp.moveaxis(o, 0, 1), s_fin


def _odd_mixer(u, conv_buf, s0, recurrent, w_in, conv_w, a_log, dt_bias, gnorm_w, w_out):
    b, t, _ = u.shape
    h = u @ w_in
    qkv, z, beta_raw, a_raw = jnp.split(
        h, [GDN_CONV_DIM, GDN_CONV_DIM + GDN_V_W, GDN_CONV_DIM + GDN_V_W + GDN_V_HEADS], axis=-1)
    qkv, new_buf = _causal_depthwise_conv(qkv, conv_buf, conv_w)
    qkv = jax.nn.silu(qkv)
    q, k, v = jnp.split(qkv, [GDN_QK_W, 2 * GDN_QK_W], axis=-1)
    rep = GDN_V_HEADS // GDN_QK_HEADS
    q = _l2_normalize(q.astype(F32).reshape(b, t, GDN_QK_HEADS, GDN_HEAD_DIM))
    k = _l2_normalize(k.astype(F32).reshape(b, t, GDN_QK_HEADS, GDN_HEAD_DIM))
    q = jnp.repeat(q, rep, axis=2) * (GDN_HEAD_DIM ** -0.5)
    k = jnp.repeat(k, rep, axis=2)
    v = v.astype(F32).reshape(b, t, GDN_V_HEADS, GDN_HEAD_DIM)
    beta = jax.nn.sigmoid(beta_raw.astype(F32))
    g = -jnp.exp(a_log.astype(F32)) * jax.nn.softplus(a_raw.astype(F32) + dt_bias.astype(F32))
    if recurrent:
        o, s_fin = _gated_delta_recurrent(q, k, v, g, beta, s0)
    else:
        o, s_fin = _gated_delta_chunked(q, k, v, g, beta, s0)
    o = o * lax.rsqrt(jnp.mean(jnp.square(o), axis=-1, keepdims=True) + RMS_EPS) * gnorm_w.astype(F32)
    o = o * jax.nn.silu(z.astype(F32).reshape(b, t, GDN_V_HEADS, GDN_HEAD_DIM))
    out = o.reshape(b, t, GDN_V_W).astype(u.dtype) @ w_out
    return out, new_buf, s_fin


def setup_inputs(seed: int = 0) -> dict:
    key = jax.random.key(seed)
    ks = jax.random.split(key, 26)
    n_pages = PAST_LEN // PAGE_SIZE
    n_used = DEC_BATCH * n_pages
    n_pool = n_used + (n_used + 3) // 4

    def nrm(k, shape, s):
        return jax.random.normal(k, shape, F32) * s

    x_prompt = nrm(ks[0], (BATCH, SEQ, D_MODEL), 1.0)
    x_sample = nrm(ks[1], (DEC_BATCH, DEC_SEQ, D_MODEL), 1.0)
    c_prompt = nrm(ks[2], (BATCH, D_MODEL), 1.0)
    c_sample = nrm(ks[3], (DEC_BATCH, D_MODEL), 1.0)
    cache_k = nrm(ks[4], (N_EVEN, n_pool, PAGE_SIZE, SB_HEADS, SB_HEAD_DIM), 1.0)
    cache_v = nrm(ks[5], (N_EVEN, n_pool, PAGE_SIZE, SB_HEADS, SB_HEAD_DIM), 1.0)
    page_table = jax.random.permutation(ks[6], n_pool)[:n_used].reshape(DEC_BATCH, n_pages).astype(jnp.int32)
    state_conv_a = nrm(ks[7], (N_EVEN, DEC_BATCH, CONV_WIDTH_A - 1, W_A), 0.5)
    state_conv_c = nrm(ks[8], (N_ODD, DEC_BATCH, GDN_CONV_WIDTH - 1, GDN_CONV_DIM), 1.0)
    state_delta = nrm(ks[9], (N_ODD, DEC_BATCH, GDN_V_HEADS, GDN_HEAD_DIM, GDN_HEAD_DIM), 0.3)
    w_ada = nrm(ks[10], (DEPTH, D_MODEL, 3 * D_MODEL), 0.5 * D_MODEL ** -0.5)
    b_ada = nrm(ks[11], (DEPTH, 3 * D_MODEL), 0.01)
    ln_g = 1.0 + nrm(ks[12], (DEPTH, D_MODEL), 0.02)
    ln_b = nrm(ks[13], (DEPTH, D_MODEL), 0.02)
    w_in_even = nrm(ks[14], (N_EVEN, D_MODEL, IN_EVEN), D_MODEL ** -0.5)
    conv_w_a = nrm(ks[15], (N_EVEN, CONV_WIDTH_A, W_A), CONV_WIDTH_A ** -0.5)
    gn_g_a = 1.0 + nrm(ks[16], (N_EVEN, W_A), 0.02)
    gn_b_a = nrm(ks[17], (N_EVEN, W_A), 0.02)
    sb_bias = SB_BIAS_INIT + nrm(ks[25], (N_EVEN, SB_HEADS), 0.5)
    w_out_even = nrm(ks[18], (N_EVEN, W_A + W_B, D_MODEL), DEEPNORM_BETA * (W_A + W_B) ** -0.5)
    w_in_odd = nrm(ks[19], (N_ODD, D_MODEL, IN_ODD), D_MODEL ** -0.5)
    conv_w_c = nrm(ks[20], (N_ODD, GDN_CONV_WIDTH, GDN_CONV_DIM), GDN_CONV_WIDTH ** -0.5)
    a_log_c = jnp.log(jax.random.uniform(ks[21], (N_ODD, GDN_V_HEADS), F32, 1.0, 16.0))
    dt = jnp.exp(jax.random.uniform(ks[22], (N_ODD, GDN_V_HEADS), F32, math.log(1e-3), math.log(1e-1)))
    dt_bias_c = dt + jnp.log(-jnp.expm1(-dt))
    gnorm_w_c = 1.0 + nrm(ks[23], (N_ODD, GDN_HEAD_DIM), 0.02)
    w_out_odd = nrm(ks[24], (N_ODD, GDN_V_W, D_MODEL), DEEPNORM_BETA * GDN_V_W ** -0.5)
    return {"x_prompt": x_prompt, "x_sample": x_sample, "c_prompt": c_prompt, "c_sample": c_sample,
            "cache_k": cache_k, "cache_v": cache_v, "page_table": page_table,
            "state_conv_a": state_conv_a, "state_conv_c": state_conv_c, "state_delta": state_delta,
            "w_ada": w_ada, "b_ada": b_ada, "ln_g": ln_g, "ln_b": ln_b,
            "w_in_even": w_in_even, "conv_w_a": conv_w_a, "gn_g_a": gn_g_a, "gn_b_a": gn_b_a,
            "sb_bias": sb_bias, "w_out_even": w_out_even, "w_in_odd": w_in_odd, "conv_w_c": conv_w_c,
            "a_log_c": a_log_c, "dt_bias_c": dt_bias_c, "gnorm_w_c": gnorm_w_c, "w_out_odd": w_out_odd}


def reference(x_prompt, x_sample, c_prompt, c_sample, cache_k, cache_v, page_table,
              state_conv_a, state_conv_c, state_delta, w_ada, b_ada, ln_g, ln_b,
              w_in_even, conv_w_a, gn_g_a, gn_b_a, sb_bias, w_out_even,
              w_in_odd, conv_w_c, a_log_c, dt_bias_c, gnorm_w_c, w_out_odd):
    n_seq, n_pages = page_table.shape
    past_len = n_pages * cache_k.shape[2]
    bp = x_prompt.shape[0]
    xp, xs = x_prompt, x_sample
    kp_l, vp_l, ks_l, vs_l = [], [], [], []
    cap_l, cas_l, ccp_l, ccs_l, dp_l, ds_l = [], [], [], [], [], []
    for layer in range(DEPTH):
        sh_p, sc_p, gt_p = _adaln(c_prompt, w_ada[layer], b_ada[layer])
        sh_s, sc_s, gt_s = _adaln(c_sample, w_ada[layer], b_ada[layer])
        up = xp * (1 + sc_p) + sh_p
        us = xs * (1 + sc_s) + sh_s
        j = layer // 2
        if layer % 2 == 0:
            prm = (w_in_even[j], conv_w_a[j], gn_g_a[j], gn_b_a[j], sb_bias[j], w_out_even[j])
            buf0 = jnp.zeros((bp, CONV_WIDTH_A - 1, W_A), xp.dtype)
            hp, buf_p, k_p, v_p = _even_mixer(up, buf0, _stick_breaking_prompt, *prm)
            k_past = cache_k[j][page_table].reshape(n_seq, past_len, SB_HEADS, SB_HEAD_DIM)
            v_past = cache_v[j][page_table].reshape(n_seq, past_len, SB_HEADS, SB_HEAD_DIM)
            hs, buf_s, k_s, v_s = _even_mixer(us, state_conv_a[j], _stick_breaking_with_past(k_past, v_past), *prm)
            kp_l.append(k_p)
            vp_l.append(v_p)
            ks_l.append(k_s)
            vs_l.append(v_s)
            cap_l.append(buf_p)
            cas_l.append(buf_s.astype(state_conv_a.dtype))
        else:
            prm = (w_in_odd[j], conv_w_c[j], a_log_c[j], dt_bias_c[j], gnorm_w_c[j], w_out_odd[j])
            buf0 = jnp.zeros((bp, GDN_CONV_WIDTH - 1, GDN_CONV_DIM), xp.dtype)
            s0 = jnp.zeros((bp, GDN_V_HEADS, GDN_HEAD_DIM, GDN_HEAD_DIM), F32)
            hp, buf_p, s_p = _odd_mixer(up, buf0, s0, False, *prm)
            hs, buf_s, s_s = _odd_mixer(us, state_conv_c[j], state_delta[j].astype(F32), True, *prm)
            ccp_l.append(buf_p)
            ccs_l.append(buf_s.astype(state_conv_c.dtype))
            dp_l.append(s_p.astype(state_delta.dtype))
            ds_l.append(s_s.astype(state_delta.dtype))
        xp = _layer_norm(DEEPNORM_ALPHA * xp + (1 + gt_p) * hp, ln_g[layer], ln_b[layer])
        xs = _layer_norm(DEEPNORM_ALPHA * xs + (1 + gt_s) * hs, ln_g[layer], ln_b[layer])
    return (xp, xs, jnp.stack(kp_l), jnp.stack(vp_l), jnp.stack(ks_l), jnp.stack(vs_l),
            jnp.stack(cap_l), jnp.stack(cas_l), jnp.stack(ccp_l), jnp.stack(ccs_l),
            jnp.stack(dp_l), jnp.stack(ds_l))
```

```python
import functools

import jax
import jax.numpy as jnp
from jax import lax
from jax.experimental import pallas as pl
from jax.experimental.pallas import tpu as pltpu

F32 = jnp.float32
BF16 = jnp.bfloat16

D_MODEL = 2048
DEPTH = 4
CONV_WIDTH_A = 31
W_A = 1024
GROUP_A = 128
SB_HEAD_DIM = 128
SB_HEADS = 8
W_B = 1024
GDN_HEAD_DIM = 128
GDN_QK_HEADS = 16
GDN_V_HEADS = 32
GDN_QK_W = 2048
GDN_V_W = 4096
GDN_CONV_WIDTH = 4
GDN_CONV_DIM = 8192
GDN_CHUNK = 128
DEEPNORM_ALPHA = (2 * DEPTH) ** 0.25
LN_EPS = 1e-5
RMS_EPS = 1e-6
SAMPLE_ROWS = 16
LANES = 128
VMEM_LIMIT = 48 * 1024 * 1024

NN = (((1,), (0,)), ((), ()))
NT = (((1,), (1,)), ((), ()))


def _params(*sem):
    return pltpu.CompilerParams(dimension_semantics=sem, vmem_limit_bytes=VMEM_LIMIT)


def _dot(a, b, dims=NN):
    return lax.dot_general(a, b, dims, preferred_element_type=F32)


def _split(x):
    hi = x.astype(BF16)
    lo = (x - hi.astype(F32)).astype(BF16)
    return hi, lo


def _dot3(a, b, dims=NN):
    ah, al = _split(a)
    bh, bl = _split(b)
    return _dot(ah, bh, dims) + (_dot(ah, bl, dims) + _dot(al, bh, dims))


def _dot_exact_rhs(a, m_bf16):
    ah, al = _split(a)
    return _dot(ah, m_bf16) + _dot(al, m_bf16)


def _sigmoid(x):
    return 1.0 / (1.0 + jnp.exp(-x))


def _silu(x):
    return x * _sigmoid(x)


def _softplus(x):
    return jnp.maximum(x, 0.0) + jnp.log1p(jnp.exp(-jnp.abs(x)))


def _ada_kernel(c_ref, w_ref, b_ref, o_ref):
    s = _silu(c_ref[...]).astype(BF16)
    o_ref[0] = _dot(s, w_ref[0].astype(BF16)) + b_ref[0]


def _ada_all(c_all, w_ada, b_ada):
    depth, d, n = w_ada.shape
    rows = c_all.shape[0]
    tn = 512
    return pl.pallas_call(
        _ada_kernel,
        grid=(depth, n // tn),
        in_specs=[pl.BlockSpec((rows, d), lambda l, j: (0, 0)),
                  pl.BlockSpec((1, d, tn), lambda l, j: (l, 0, j)),
                  pl.BlockSpec((1, 1, tn), lambda l, j: (l, 0, j))],
        out_specs=pl.BlockSpec((1, rows, tn), lambda l, j: (l, 0, j)),
        out_shape=jax.ShapeDtypeStruct((depth, rows, n), F32),
        compiler_params=_params("arbitrary", "arbitrary"),
        name="ada_mod",
    )(c_all, w_ada, b_ada.reshape(depth, 1, n))


def _modulate_kernel(x_ref, sc_ref, sh_ref, u_ref):
    u_ref[...] = (x_ref[...] * (1.0 + sc_ref[0]) + sh_ref[0]).astype(u_ref.dtype)


def _modulate(x, sc, sh, rows_per_mod, tm):
    m, d = x.shape
    r = sc.shape[1]
    per = rows_per_mod // tm
    return pl.pallas_call(
        _modulate_kernel,
        grid=(m // tm,),
        in_specs=[pl.BlockSpec((tm, d), lambda i: (i, 0)),
                  pl.BlockSpec((1, r, d), lambda i: (i // per, 0, 0)),
                  pl.BlockSpec((1, r, d), lambda i: (i // per, 0, 0))],
        out_specs=pl.BlockSpec((tm, d), lambda i: (i, 0)),
        out_shape=jax.ShapeDtypeStruct((m, d), BF16),
        compiler_params=_params("arbitrary"),
        name="modulate",
    )(x, sc, sh)


def _matmul_kernel(a_ref, w_ref, o_ref):
    o_ref[...] = _dot(a_ref[...], w_ref[...]).astype(o_ref.dtype)


def _matmul(a, w, n_out, tm, tn):
    m, k = a.shape
    return pl.pallas_call(
        _matmul_kernel,
        grid=(m // tm, n_out // tn),
        in_specs=[pl.BlockSpec((tm, k), lambda i, j: (i, 0)),
                  pl.BlockSpec((k, tn), lambda i, j: (0, j))],
        out_specs=pl.BlockSpec((tm, tn), lambda i, j: (i, j)),
        out_shape=jax.ShapeDtypeStruct((m, n_out), F32),
        compiler_params=_params("arbitrary", "arbitrary"),
        name="in_proj",
    )(a, w)


def _post_norm(h, x, gt, g, b):
    r = DEEPNORM_ALPHA * x + (1.0 + gt) * h
    mu = jnp.mean(r, axis=-1, keepdims=True)
    rc = r - mu
    var = jnp.mean(rc * rc, axis=-1, keepdims=True)
    return rc * lax.rsqrt(var + LN_EPS) * g + b


def _out_even_kernel(ya_ref, yb_ref, w_ref, x_ref, gt_ref, g_ref, b_ref, sc_ref, sh_ref, xo_ref, uo_ref):
    ka = ya_ref.shape[1]
    h = _dot(ya_ref[...], w_ref[0:ka, :]) + _dot(yb_ref[...], w_ref[ka:, :])
    y = _post_norm(h, x_ref[...], gt_ref[0], g_ref[...], b_ref[...])
    xo_ref[...] = y
    uo_ref[...] = (y * (1.0 + sc_ref[0]) + sh_ref[0]).astype(uo_ref.dtype)


def _out_even(ya, yb, w, x, gt, g, b, sc, sh, rows_per_mod, tm):
    m, d = x.shape
    ka, kb = ya.shape[1], yb.shape[1]
    r = gt.shape[1]
    per = rows_per_mod // tm
    mod = pl.BlockSpec((1, r, d), lambda i: (i // per, 0, 0))
    vec = pl.BlockSpec((1, d), lambda i: (0, 0))
    return pl.pallas_call(
        _out_even_kernel,
        grid=(m // tm,),
        in_specs=[pl.BlockSpec((tm, ka), lambda i: (i, 0)),
                  pl.BlockSpec((tm, kb), lambda i: (i, 0)),
                  pl.BlockSpec((ka + kb, d), lambda i: (0, 0)),
                  pl.BlockSpec((tm, d), lambda i: (i, 0)),
                  mod, vec, vec, mod, mod],
        out_specs=[pl.BlockSpec((tm, d), lambda i: (i, 0)),
                   pl.BlockSpec((tm, d), lambda i: (i, 0))],
        out_shape=[jax.ShapeDtypeStruct((m, d), F32), jax.ShapeDtypeStruct((m, d), BF16)],
        compiler_params=_params("arbitrary"),
        name="out_proj_even",
    )(ya, yb, w, x, gt, g, b, sc, sh)


def _out_odd_kernel(a_ref, w_ref, x_ref, gt_ref, g_ref, b_ref, sc_ref, sh_ref, xo_ref, uo_ref, acc_ref):
    k = pl.program_id(1)

    @pl.when(k == 0)
    def _():
        acc_ref[...] = jnp.zeros_like(acc_ref)

    acc_ref[...] += _dot(a_ref[...], w_ref[...])

    @pl.when(k == pl.num_programs(1) - 1)
    def _():
        y = _post_norm(acc_ref[...], x_ref[...], gt_ref[0], g_ref[...], b_ref[...])
        xo_ref[...] = y
        uo_ref[...] = (y * (1.0 + sc_ref[0]) + sh_ref[0]).astype(uo_ref.dtype)


def _out_odd(a, w, x, gt, g, b, sc, sh, rows_per_mod, tm, tk):
    m, d = x.shape
    kdim = a.shape[1]
    r = gt.shape[1]
    per = rows_per_mod // tm
    mod = pl.BlockSpec((1, r, d), lambda i, k: (i // per, 0, 0))
    vec = pl.BlockSpec((1, d), lambda i, k: (0, 0))
    return pl.pallas_call(
        _out_odd_kernel,
        grid=(m // tm, kdim // tk),
        in_specs=[pl.BlockSpec((tm, tk), lambda i, k: (i, k)),
                  pl.BlockSpec((tk, d), lambda i, k: (k, 0)),
                  pl.BlockSpec((tm, d), lambda i, k: (i, 0)),
                  mod, vec, vec, mod, mod],
        out_specs=[pl.BlockSpec((tm, d), lambda i, k: (i, 0)),
                   pl.BlockSpec((tm, d), lambda i, k: (i, 0))],
        out_shape=[jax.ShapeDtypeStruct((m, d), F32), jax.ShapeDtypeStruct((m, d), BF16)],
        scratch_shapes=[pltpu.VMEM((tm, d), F32)],
        compiler_params=_params("arbitrary", "arbitrary"),
        name="out_proj_odd",
    )(a, w, x, gt, g, b, sc, sh)


CONV_A_HALO = 32
CONV_A_ROWS = 64


def _group_norm_gate(y, gate, g, b):
    outs = []
    for s in range(y.shape[1] // GROUP_A):
        sl = slice(s * GROUP_A, (s + 1) * GROUP_A)
        seg = y[:, sl]
        mu = jnp.mean(seg, axis=-1, keepdims=True)
        sc = seg - mu
        var = jnp.mean(sc * sc, axis=-1, keepdims=True)
        yn = sc * lax.rsqrt(var + LN_EPS) * g[:, sl] + b[:, sl]
        outs.append(_silu(yn) * _silu(gate[:, sl]))
    return outs[0] if len(outs) == 1 else jnp.concatenate(outs, axis=1)


def _conv_a_kernel(val_ref, glu_ref, gate_ref, w_ref, g_ref, b_ref, y_ref, st_ref, ext_ref):
    t = pl.program_id(2)
    tt = val_ref.shape[0]
    kw = w_ref.shape[0]
    off = CONV_A_HALO - (kw - 1)

    @pl.when(t == 0)
    def _():
        ext_ref[0:CONV_A_HALO, :] = jnp.zeros((CONV_A_HALO, ext_ref.shape[1]), F32)

    ext_ref[CONV_A_HALO:, :] = val_ref[...] * _sigmoid(glu_ref[...])
    for r0 in range(0, tt, CONV_A_ROWS):
        acc = w_ref[0:1, :] * ext_ref[r0 + off:r0 + off + CONV_A_ROWS, :]
        for j in range(1, kw):
            acc = acc + w_ref[j:j + 1, :] * ext_ref[r0 + off + j:r0 + off + j + CONV_A_ROWS, :]
        y = _group_norm_gate(acc, gate_ref[r0:r0 + CONV_A_ROWS, :], g_ref[...], b_ref[...])
        y_ref[r0:r0 + CONV_A_ROWS, :] = y.astype(y_ref.dtype)

    @pl.when(t == pl.num_programs(2) - 1)
    def _():
        st_ref[0] = ext_ref[tt + off:tt + CONV_A_HALO, :]

    ext_ref[0:CONV_A_HALO, :] = ext_ref[tt:tt + CONV_A_HALO, :]


def _conv_a_prompt(h, conv_w, gn_g, gn_b, batch, seq, tt=256, tc=256):
    nt = seq // tt
    nc = W_A // tc
    kw = conv_w.shape[0]
    col = lambda base: pl.BlockSpec((tt, tc), lambda b, c, t: (b * nt + t, base * nc + c))
    vec = pl.BlockSpec((1, tc), lambda b, c, t: (0, c))
    return pl.pallas_call(
        _conv_a_kernel,
        grid=(batch, nc, nt),
        in_specs=[col(0), col(1), col(2),
                  pl.BlockSpec((kw, tc), lambda b, c, t: (0, c)), vec, vec],
        out_specs=[pl.BlockSpec((tt, tc), lambda b, c, t: (b * nt + t, c)),
                   pl.BlockSpec((1, kw - 1, tc), lambda b, c, t: (b, 0, c))],
        out_shape=[jax.ShapeDtypeStruct((batch * seq, W_A), BF16),
                   jax.ShapeDtypeStruct((batch, kw - 1, W_A), F32)],
        scratch_shapes=[pltpu.VMEM((CONV_A_HALO + tt, tc), F32)],
        compiler_params=_params("arbitrary", "arbitrary", "arbitrary"),
        name="conv_a_prompt",
    )(h, h, h, conv_w, gn_g.reshape(1, -1), gn_b.reshape(1, -1))


def _conv_a_sample_kernel(h_ref, st_ref, w_ref, g_ref, b_ref, y_ref, so_ref):
    nseq, kw1, _ = st_ref.shape
    glu = h_ref[:, 0:W_A] * _sigmoid(h_ref[:, W_A:2 * W_A])
    rows = []
    for s in range(nseq):
        conv = jnp.sum(st_ref[s] * w_ref[0:kw1, :], axis=0, keepdims=True) + glu[s:s + 1, :] * w_ref[kw1:kw1 + 1, :]
        rows.append(conv)
        so_ref[s, 0:kw1 - 1, :] = st_ref[s, 1:kw1, :]
        so_ref[s, kw1 - 1:kw1, :] = glu[s:s + 1, :]
    rows.append(jnp.zeros((h_ref.shape[0] - nseq, W_A), F32))
    conv = jnp.concatenate(rows, axis=0)
    y = _group_norm_gate(conv, h_ref[:, 2 * W_A:3 * W_A], g_ref[...], b_ref[...])
    y_ref[...] = y.astype(y_ref.dtype)


def _conv_a_sample(h_s, state, conv_w, gn_g, gn_b):
    rows = h_s.shape[0]
    return pl.pallas_call(
        _conv_a_sample_kernel,
        out_shape=[jax.ShapeDtypeStruct((rows, W_A), BF16), jax.ShapeDtypeStruct(state.shape, F32)],
        compiler_params=pltpu.CompilerParams(vmem_limit_bytes=VMEM_LIMIT),
        name="conv_a_sample",
    )(h_s, state, conv_w, gn_g.reshape(1, -1), gn_b.reshape(1, -1))


def _attn_prompt_kernel(bias_ref, q_ref, k_ref, v_ref, gate_ref, o_ref, acc_ref, later_ref):
    h = pl.program_id(1)
    i = pl.program_id(2)
    blk = q_ref.shape[0]
    bias = bias_ref[h]
    q = (q_ref[...] * (SB_HEAD_DIM ** -0.5)).astype(BF16)
    row = lax.broadcasted_iota(jnp.int32, (blk, blk), 0)
    col = lax.broadcasted_iota(jnp.int32, (blk, blk), 1)
    after = jnp.where(row > col, 1.0, 0.0).astype(BF16)
    visible = col < row

    acc_ref[...] = jnp.zeros_like(acc_ref)
    later_ref[...] = jnp.zeros_like(later_ref)

    def block(j, masked):
        start = pl.multiple_of(j * blk, blk)
        ks = k_ref[pl.ds(start, blk), :].astype(BF16)
        vs = v_ref[pl.ds(start, blk), :].astype(BF16)
        z = _dot(q, ks, NT) + bias
        sp = _softplus(z)
        if masked:
            sp = jnp.where(visible, sp, 0.0)
        inside = _dot_exact_rhs(sp, after)
        w = jnp.exp(z - sp - inside - later_ref[...])
        if masked:
            w = jnp.where(visible, w, 0.0)
        acc_ref[...] += _dot(w.astype(BF16), vs)
        later_ref[...] += jnp.sum(sp, axis=-1, keepdims=True)

    block(i, True)

    def body(jj, carry):
        block(i - 1 - jj, False)
        return carry

    lax.fori_loop(0, i, body, 0)
    o_ref[...] = (acc_ref[...] * _silu(gate_ref[...])).astype(o_ref.dtype)


def _attn_prompt(h, sb_bias, batch, seq, blk=256):
    nq = seq // blk
    base = 3 * W_A // SB_HEAD_DIM
    return pl.pallas_call(
        _attn_prompt_kernel,
        grid_spec=pltpu.PrefetchScalarGridSpec(
            num_scalar_prefetch=0,
            grid=(batch, SB_HEADS, nq),
            in_specs=[pl.BlockSpec(memory_space=pltpu.SMEM),
                      pl.BlockSpec((blk, SB_HEAD_DIM), lambda b, hh, i: (b * nq + i, base + hh)),
                      pl.BlockSpec((seq, SB_HEAD_DIM), lambda b, hh, i: (b, base + SB_HEADS + hh)),
                      pl.BlockSpec((seq, SB_HEAD_DIM), lambda b, hh, i: (b, base + 2 * SB_HEADS + hh)),
                      pl.BlockSpec((blk, SB_HEAD_DIM), lambda b, hh, i: (b * nq + i, base + 3 * SB_HEADS + hh))],
            out_specs=pl.BlockSpec((blk, SB_HEAD_DIM), lambda b, hh, i: (b * nq + i, hh)),
            scratch_shapes=[pltpu.VMEM((blk, SB_HEAD_DIM), F32), pltpu.VMEM((blk, 1), F32)]),
        out_shape=jax.ShapeDtypeStruct((batch * seq, W_B), BF16),
        compiler_params=_params("arbitrary", "arbitrary", "arbitrary"),
        name="attn_prompt",
    )(sb_bias, h, h, h, h)


PAGES_PER_STEP = 4


def _attn_sample_kernel(pt_ref, q_ref, bias_ref, gate_ref, *refs):
    npg = PAGES_PER_STEP
    k_refs, v_refs = refs[:npg], refs[npg:2 * npg]
    o_ref, acc_ref, later_ref = refs[2 * npg:]
    p = pl.program_id(1)
    rows = q_ref.shape[1]
    page = k_refs[0].shape[2] // SB_HEADS

    @pl.when(p == 0)
    def _():
        acc_ref[...] = jnp.zeros_like(acc_ref)
        later_ref[...] = jnp.zeros_like(later_ref)

    q = (q_ref[0] * (SB_HEAD_DIM ** -0.5)).astype(BF16)
    rid = lax.broadcasted_iota(jnp.int32, (rows, page), 0)
    r2 = lax.broadcasted_iota(jnp.int32, (page, page), 0)
    c2 = lax.broadcasted_iota(jnp.int32, (page, page), 1)
    after = jnp.where(r2 > c2, 1.0, 0.0).astype(BF16)
    for r in range(npg):
        z = jnp.zeros((rows, page), F32)
        for hh in range(SB_HEADS):
            kh = k_refs[r][0, 0, pl.ds(hh, page, stride=SB_HEADS), :].astype(BF16)
            z = jnp.where(rid == hh, _dot(q, kh, NT), z)
        z = z + bias_ref[...]
        sp = _softplus(z)
        inside = _dot_exact_rhs(sp, after)
        w = jnp.exp(z - sp - inside - later_ref[...]).astype(BF16)
        o = jnp.zeros((rows, SB_HEAD_DIM), F32)
        for hh in range(SB_HEADS):
            vh = v_refs[r][0, 0, pl.ds(hh, page, stride=SB_HEADS), :].astype(BF16)
            o = jnp.where(rid == hh, _dot(w, vh), o)
        acc_ref[...] += o
        later_ref[...] += jnp.sum(sp, axis=-1, keepdims=True)

    @pl.when(p == pl.num_programs(1) - 1)
    def _():
        o_ref[0] = acc_ref[...] * _silu(gate_ref[0])


def _attn_sample(q, gate, bias, cache_k, cache_v, page_table, layer):
    nseq, rows, _ = q.shape
    n_pages = page_table.shape[1]
    npg = PAGES_PER_STEP
    blk = cache_k.shape[2]

    def page_spec(r):
        return pl.BlockSpec((1, 1, blk, SB_HEAD_DIM),
                            lambda s, p, pt: (layer, pt[s, n_pages - 1 - (p * npg + r)], 0, 0))

    row_spec = pl.BlockSpec((1, rows, SB_HEAD_DIM), lambda s, p, pt: (s, 0, 0))
    return pl.pallas_call(
        _attn_sample_kernel,
        grid_spec=pltpu.PrefetchScalarGridSpec(
            num_scalar_prefetch=1,
            grid=(nseq, n_pages // npg),
            in_specs=[row_spec, pl.BlockSpec((rows, 1), lambda s, p, pt: (0, 0)), row_spec]
                     + [page_spec(r) for r in range(npg)] + [page_spec(r) for r in range(npg)],
            out_specs=row_spec,
            scratch_shapes=[pltpu.VMEM((rows, SB_HEAD_DIM), F32), pltpu.VMEM((rows, 1), F32)]),
        out_shape=jax.ShapeDtypeStruct((nseq, rows, SB_HEAD_DIM), F32),
        compiler_params=_params("arbitrary", "arbitrary"),
        name="attn_sample",
    )(page_table, q, bias, gate, *([cache_k] * npg), *([cache_v] * npg))


def _gdn_gate_kernel(u_ref, wt_ref, alog_ref, dt_ref, beta_ref, gc_ref):
    raw = _dot(wt_ref[...], u_ref[...], NT)
    hv = raw.shape[0] // 2
    beta_ref[0] = _sigmoid(raw[0:hv, :])
    g = -jnp.exp(alog_ref[...]) * _softplus(raw[hv:, :] + dt_ref[...])
    r2 = lax.broadcasted_iota(jnp.int32, (GDN_CHUNK, GDN_CHUNK), 0)
    c2 = lax.broadcasted_iota(jnp.int32, (GDN_CHUNK, GDN_CHUNK), 1)
    upto = jnp.where(r2 <= c2, 1.0, 0.0).astype(BF16)
    gh, gl = _split(g)
    gl2 = (g - gh.astype(F32) - gl.astype(F32)).astype(BF16)
    for c in range(g.shape[1] // GDN_CHUNK):
        sl = slice(c * GDN_CHUNK, (c + 1) * GDN_CHUNK)
        gc_ref[0, :, sl] = _dot(gh[:, sl], upto) + (_dot(gl[:, sl], upto) + _dot(gl2[:, sl], upto))


def _gdn_gates(u, w_ba_t, a_log, dt_bias, batch, seq, tt=512):
    nt = seq // tt
    hv2, d = w_ba_t.shape
    hv = hv2 // 2
    out = pl.BlockSpec((1, hv, tt), lambda b, t: (b, 0, t))
    colv = pl.BlockSpec((hv, 1), lambda b, t: (0, 0))
    return pl.pallas_call(
        _gdn_gate_kernel,
        grid=(batch, nt),
        in_specs=[pl.BlockSpec((tt, d), lambda b, t: (b * nt + t, 0)),
                  pl.BlockSpec((hv2, d), lambda b, t: (0, 0)), colv, colv],
        out_specs=[out, out],
        out_shape=[jax.ShapeDtypeStruct((batch, hv, seq), F32)] * 2,
        compiler_params=_params("arbitrary", "arbitrary"),
        name="gdn_gates",
    )(u, w_ba_t, a_log.reshape(hv, 1), dt_bias.reshape(hv, 1))


CONV_C_HALO = 8


def _qk_norm(y, cblk, tc):
    first = cblk * tc
    is_q = first < GDN_QK_W
    is_qk = first < 2 * GDN_QK_W
    outs = []
    for s in range(tc // GDN_HEAD_DIM):
        seg = y[:, s * GDN_HEAD_DIM:(s + 1) * GDN_HEAD_DIM]
        rs = lax.rsqrt(jnp.sum(seg * seg, axis=-1, keepdims=True) + RMS_EPS)
        f = jnp.where(is_q, rs * (GDN_HEAD_DIM ** -0.5), jnp.where(is_qk, rs, 1.0))
        outs.append(seg * f)
    return outs[0] if len(outs) == 1 else jnp.concatenate(outs, axis=1)


def _conv_c_kernel(x_ref, w_ref, y_ref, st_ref, ext_ref):
    c = pl.program_id(1)
    t = pl.program_id(2)
    tt, tc = x_ref.shape
    kw = w_ref.shape[0]
    off = CONV_C_HALO - (kw - 1)

    @pl.when(t == 0)
    def _():
        ext_ref[0:CONV_C_HALO, :] = jnp.zeros((CONV_C_HALO, tc), F32)

    ext_ref[CONV_C_HALO:, :] = x_ref[...]
    acc = w_ref[0:1, :] * ext_ref[off:off + tt, :]
    for j in range(1, kw):
        acc = acc + w_ref[j:j + 1, :] * ext_ref[off + j:off + j + tt, :]
    y_ref[...] = _qk_norm(_silu(acc), c, tc)

    @pl.when(t == pl.num_programs(2) - 1)
    def _():
        st_ref[0] = ext_ref[tt + off:tt + CONV_C_HALO, :]

    ext_ref[0:CONV_C_HALO, :] = ext_ref[tt:tt + CONV_C_HALO, :]


def _conv_c_prompt(h, conv_w, batch, seq, tt=256, tc=512):
    nt = seq // tt
    nc = GDN_CONV_DIM // tc
    kw = conv_w.shape[0]
    return pl.pallas_call(
        _conv_c_kernel,
        grid=(batch, nc, nt),
        in_specs=[pl.BlockSpec((tt, tc), lambda b, c, t: (b * nt + t, c)),
                  pl.BlockSpec((kw, tc), lambda b, c, t: (0, c))],
        out_specs=[pl.BlockSpec((tt, tc), lambda b, c, t: (b * nt + t, c)),
                   pl.BlockSpec((1, kw - 1, tc), lambda b, c, t: (b, 0, c))],
        out_shape=[jax.ShapeDtypeStruct((batch * seq, GDN_CONV_DIM), F32),
                   jax.ShapeDtypeStruct((batch, kw - 1, GDN_CONV_DIM), F32)],
        scratch_shapes=[pltpu.VMEM((CONV_C_HALO + tt, tc), F32)],
        compiler_params=_params("arbitrary", "arbitrary", "arbitrary"),
        name="conv_c_prompt",
    )(h, conv_w)


def _gdn_prompt_kernel(q_ref, k_ref, v_ref, z_ref, beta_ref, gc_ref, gw_ref, o_ref, s_out_ref, s_ref):
    t = pl.program_id(2)
    c = GDN_CHUNK
    tt = q_ref.shape[0]
    rep = v_ref.shape[1] // GDN_HEAD_DIM

    @pl.when(t == 0)
    def _():
        s_ref[...] = jnp.zeros_like(s_ref)

    row = lax.broadcasted_iota(jnp.int32, (c, c), 0)
    col = lax.broadcasted_iota(jnp.int32, (c, c), 1)
    incl = row >= col
    strict = row > col
    eye = jnp.where(row == col, 1.0, 0.0)
    merge = []
    m = 1
    while m < c:
        bi, bj = row // m, col // m
        merge.append(jnp.logical_and(bi - bj == 1, bj % 2 == 0))
        m *= 2

    for n in range(tt // c):
        rs = slice(n * c, (n + 1) * c)
        q = q_ref[rs, :]
        k = k_ref[rs, :]
        kk = _dot3(k, k, NT)
        qk = _dot3(q, k, NT)
        for hv in range(rep):
            cs = slice(hv * GDN_HEAD_DIM, (hv + 1) * GDN_HEAD_DIM)
            v = v_ref[rs, cs]
            g_row = jnp.broadcast_to(gc_ref[0, hv, :, rs], (c, c))
            b_row = jnp.broadcast_to(beta_ref[0, hv, :, rs], (c, c))
            g_col = g_row.T
            b_col = b_row.T
            decay = jnp.where(incl, jnp.exp(jnp.where(incl, g_col - g_row, 0.0)), 0.0)
            lmat = jnp.where(strict, kk * b_col * decay, 0.0)
            tmat = eye - jnp.where(merge[0], lmat, 0.0)
            for lvl in merge[1:]:
                tmat = tmat - _dot3(_dot3(tmat, jnp.where(lvl, lmat, 0.0)), tmat)
            tb = tmat * b_row
            u_val = _dot3(tb, v)
            w_key = _dot3(tb * jnp.exp(g_row), k)
            intra = jnp.where(incl, qk * decay, 0.0)
            g_last = g_col[c - 1:c, :]
            q_dec = q * jnp.exp(g_col)
            k_dec = k * jnp.exp(g_last - g_col)
            s = s_ref[hv]
            v_new = u_val - _dot3(w_key, s)
            o = _dot3(q_dec, s) + _dot3(intra, v_new)
            s_ref[hv] = s * jnp.exp(g_last) + _dot3(k_dec.T, v_new)
            o = o * lax.rsqrt(jnp.mean(o * o, axis=-1, keepdims=True) + RMS_EPS) * gw_ref[...]
            o_ref[rs, cs] = (o * _silu(z_ref[rs, cs])).astype(o_ref.dtype)

    @pl.when(t == pl.num_programs(2) - 1)
    def _():
        s_out_ref[0] = s_ref[...]


def _gdn_prompt(qkv, h, beta, gc, gnorm_w, batch, seq, tt=256):
    nt = seq // tt
    rep = GDN_V_HEADS // GDN_QK_HEADS
    dh = GDN_HEAD_DIM
    vw = rep * dh
    gate = pl.BlockSpec((1, rep, 1, tt), lambda b, hq, t: (b, hq, 0, t))
    return pl.pallas_call(
        _gdn_prompt_kernel,
        grid=(batch, GDN_QK_HEADS, nt),
        in_specs=[pl.BlockSpec((tt, dh), lambda b, hq, t: (b * nt + t, hq)),
                  pl.BlockSpec((tt, dh), lambda b, hq, t: (b * nt + t, GDN_QK_HEADS + hq)),
                  pl.BlockSpec((tt, vw), lambda b, hq, t: (b * nt + t, 2 * GDN_QK_W // vw + hq)),
                  pl.BlockSpec((tt, vw), lambda b, hq, t: (b * nt + t, GDN_CONV_DIM // vw + hq)),
                  gate, gate,
                  pl.BlockSpec((1, dh), lambda b, hq, t: (0, 0))],
        out_specs=[pl.BlockSpec((tt, vw), lambda b, hq, t: (b * nt + t, hq)),
                   pl.BlockSpec((1, rep, dh, dh), lambda b, hq, t: (b, hq, 0, 0))],
        out_shape=[jax.ShapeDtypeStruct((batch * seq, GDN_V_W), BF16),
                   jax.ShapeDtypeStruct((batch, GDN_V_HEADS, dh, dh), F32)],
        scratch_shapes=[pltpu.VMEM((rep, dh, dh), F32)],
        compiler_params=_params("arbitrary", "arbitrary", "arbitrary"),
        name="gdn_prompt",
    )(qkv, qkv, qkv, h, beta, gc, gnorm_w.reshape(1, dh))


def _conv_c_sample_kernel(h_ref, st_ref, w_ref, u_ref, wba_ref, alog_ref, dt_ref, y_ref, so_ref, beta_ref, eg_ref):
    nseq, kw1, width = st_ref.shape
    tc = 1024
    for c0 in range(0, width, tc):
        cs = slice(c0, c0 + tc)
        rows = []
        for s in range(nseq):
            x_new = h_ref[s:s + 1, cs]
            conv = jnp.sum(st_ref[s, :, cs] * w_ref[0:kw1, cs], axis=0, keepdims=True) + x_new * w_ref[kw1:kw1 + 1, cs]
            rows.append(conv)
            so_ref[s, 0:kw1 - 1, cs] = st_ref[s, 1:kw1, cs]
            so_ref[s, kw1 - 1:kw1, cs] = x_new
        rows.append(jnp.zeros((h_ref.shape[0] - nseq, tc), F32))
        y_ref[:, cs] = _qk_norm(_silu(jnp.concatenate(rows, axis=0)), c0 // tc, tc)
    raw = _dot(u_ref[...], wba_ref[...])
    hv = raw.shape[1] // 2
    beta_ref[...] = _sigmoid(raw[:, 0:hv])
    eg_ref[...] = jnp.exp(-jnp.exp(alog_ref[...]) * _softplus(raw[:, hv:] + dt_ref[...]))


def _conv_c_sample(h_s, state, conv_w, u_s, w_ba, a_log, dt_bias):
    rows = h_s.shape[0]
    hv = a_log.shape[0]
    return pl.pallas_call(
        _conv_c_sample_kernel,
        out_shape=[jax.ShapeDtypeStruct((rows, GDN_CONV_DIM), F32), jax.ShapeDtypeStruct(state.shape, F32),
                   jax.ShapeDtypeStruct((rows, hv), F32), jax.ShapeDtypeStruct((rows, hv), F32)],
        compiler_params=pltpu.CompilerParams(vmem_limit_bytes=VMEM_LIMIT),
        name="conv_c_sample",
    )(h_s, state, conv_w, u_s, w_ba, a_log.reshape(1, hv), dt_bias.reshape(1, hv))


def _gdn_sample_kernel(beta_ref, eg_ref, qc_ref, kc_ref, v_ref, z_ref, gw_ref, s_ref, o_ref, so_ref):
    sq = pl.program_id(0)
    rep = GDN_V_HEADS // GDN_QK_HEADS
    for hv in range(GDN_V_HEADS):
        hq = hv // rep
        k_col = kc_ref[0, :, hq:hq + 1]
        q_col = qc_ref[0, :, hq:hq + 1]
        s = s_ref[0, hv] * eg_ref[sq, hv]
        delta = (v_ref[0, hv:hv + 1, :] - jnp.sum(s * k_col, axis=0, keepdims=True)) * beta_ref[sq, hv]
        s = s + k_col * delta
        so_ref[0, hv] = s
        o = jnp.sum(s * q_col, axis=0, keepdims=True)
        o = o * lax.rsqrt(jnp.mean(o * o, axis=-1, keepdims=True) + RMS_EPS) * gw_ref[...]
        o_ref[0, hv:hv + 1, :] = o * _silu(z_ref[0, hv:hv + 1, :])


def _gdn_sample(beta, eg, q_col, k_col, v, z, gnorm_w, state):
    nseq, hv, dk, dv = state.shape
    hq = q_col.shape[2]
    smem = pl.BlockSpec(memory_space=pltpu.SMEM)
    colspec = pl.BlockSpec((1, dk, hq), lambda s: (s, 0, 0))
    rowspec = pl.BlockSpec((1, hv, dv), lambda s: (s, 0, 0))
    stspec = pl.BlockSpec((1, hv, dk, dv), lambda s: (s, 0, 0, 0))
    return pl.pallas_call(
        _gdn_sample_kernel,
        grid=(nseq,),
        in_specs=[smem, smem, colspec, colspec, rowspec, rowspec,
                  pl.BlockSpec((1, dv), lambda s: (0, 0)), stspec],
        out_specs=[rowspec, stspec],
        out_shape=[jax.ShapeDtypeStruct((nseq, hv, dv), F32), jax.ShapeDtypeStruct(state.shape, F32)],
        compiler_params=_params("arbitrary"),
        name="gdn_sample",
    )(beta, eg, q_col, k_col, v, z, gnorm_w.reshape(1, dv), state)


def kernel(x_prompt, x_sample, c_prompt, c_sample, cache_k, cache_v, page_table, state_conv_a, state_conv_c, state_delta, w_ada, b_ada, ln_g, ln_b, w_in_even, conv_w_a, gn_g_a, gn_b_a, sb_bias, w_out_even, w_in_odd, conv_w_c, a_log_c, dt_bias_c, gnorm_w_c, w_out_odd):
    batch, seq, d = x_prompt.shape
    nseq = x_sample.shape[0]
    n_even, n_pool, page_size = cache_k.shape[:3]
    np_rows = batch * seq
    pad = SAMPLE_ROWS - nseq

    c_all = jnp.concatenate([c_prompt, c_sample, jnp.zeros((SAMPLE_ROWS - batch - nseq, d), F32)], axis=0)
    mod = _ada_all(c_all, w_ada, b_ada)

    def mods(layer):
        m = mod[layer]
        parts = [m[:, i * d:(i + 1) * d] for i in range(3)]
        pm = [p[0:batch].reshape(batch, 1, d) for p in parts]
        sm = [jnp.pad(p[batch:batch + nseq], ((0, pad), (0, 0))).reshape(1, SAMPLE_ROWS, d) for p in parts]
        return pm, sm

    xp = x_prompt.reshape(np_rows, d)
    xs = jnp.pad(x_sample.reshape(nseq, d), ((0, pad), (0, 0)))
    (sh_p, sc_p, _), (sh_s, sc_s, _) = mods(0)
    up = _modulate(xp, sc_p, sh_p, seq, 512)
    us = _modulate(xs, sc_s, sh_s, SAMPLE_ROWS, SAMPLE_ROWS)

    ck = cache_k.reshape(n_even, n_pool, page_size * SB_HEADS, SB_HEAD_DIM)
    cv = cache_v.reshape(n_even, n_pool, page_size * SB_HEADS, SB_HEAD_DIM)
    bias_pad = lambda b: jnp.pad(b, (0, SAMPLE_ROWS - SB_HEADS)).reshape(SAMPLE_ROWS, 1)
    heads_pad = lambda a: jnp.pad(a.reshape(nseq, SB_HEADS, SB_HEAD_DIM), ((0, 0), (0, SAMPLE_ROWS - SB_HEADS), (0, 0)))

    kp_l, vp_l, ks_l, vs_l = [], [], [], []
    cap_l, cas_l, ccp_l, ccs_l, dp_l, ds_l = [], [], [], [], [], []
    for layer in range(DEPTH):
        j = layer // 2
        (_, _, gt_p), (_, _, gt_s) = mods(layer)
        if layer + 1 < DEPTH:
            (shn_p, scn_p, _), (shn_s, scn_s, _) = mods(layer + 1)
        else:
            shn_p = scn_p = jnp.zeros((batch, 1, d), F32)
            shn_s = scn_s = jnp.zeros((1, SAMPLE_ROWS, d), F32)
        g_ln = ln_g[layer].reshape(1, d)
        b_ln = ln_b[layer].reshape(1, d)
        if layer % 2 == 0:
            w_in = w_in_even[j].astype(BF16)
            w_out = w_out_even[j].astype(BF16)
            n_in = w_in.shape[1]
            hp = _matmul(up, w_in, n_in, min(1024, np_rows), 1024)
            ya_p, buf_p = _conv_a_prompt(hp, conv_w_a[j], gn_g_a[j], gn_b_a[j], batch, seq)
            yb_p = _attn_prompt(hp, sb_bias[j], batch, seq)
            xp, up = _out_even(ya_p, yb_p, w_out, xp, gt_p, g_ln, b_ln, scn_p, shn_p, seq, 256)
            kp_l.append(hp[:, 3 * W_A + W_B:3 * W_A + 2 * W_B].reshape(batch, seq, SB_HEADS, SB_HEAD_DIM))
            vp_l.append(hp[:, 3 * W_A + 2 * W_B:3 * W_A + 3 * W_B].reshape(batch, seq, SB_HEADS, SB_HEAD_DIM))
            cap_l.append(buf_p)
            hs = _matmul(us, w_in, n_in, SAMPLE_ROWS, 1024)
            ya_s, buf_s = _conv_a_sample(hs, state_conv_a[j], conv_w_a[j], gn_g_a[j], gn_b_a[j])
            q_s = heads_pad(hs[:nseq, 3 * W_A:3 * W_A + W_B])
            gate_s = heads_pad(hs[:nseq, 3 * W_A + 3 * W_B:3 * W_A + 4 * W_B])
            o_s = _attn_sample(q_s, gate_s, bias_pad(sb_bias[j]), ck, cv, page_table, j)
            yb_s = jnp.pad(o_s[:, :SB_HEADS].reshape(nseq, W_B), ((0, pad), (0, 0))).astype(BF16)
            xs, us = _out_even(ya_s, yb_s, w_out, xs, gt_s, g_ln, b_ln, scn_s, shn_s, SAMPLE_ROWS, SAMPLE_ROWS)
            ks_l.append(hs[:nseq, 3 * W_A + W_B:3 * W_A + 2 * W_B].reshape(nseq, 1, SB_HEADS, SB_HEAD_DIM))
            vs_l.append(hs[:nseq, 3 * W_A + 2 * W_B:3 * W_A + 3 * W_B].reshape(nseq, 1, SB_HEADS, SB_HEAD_DIM))
            cas_l.append(buf_s)
        else:
            w_in = w_in_odd[j].astype(BF16)
            w_out = w_out_odd[j].astype(BF16)
            n_main = GDN_CONV_DIM + GDN_V_W
            w_ba = w_in[:, n_main:]
            hp = _matmul(up, w_in, n_main, min(1024, np_rows), 1024)
            beta_p, gc_p = _gdn_gates(up, w_ba.T, a_log_c[j], dt_bias_c[j], batch, seq)
            qkv_p, buf_p = _conv_c_prompt(hp, conv_w_c[j], batch, seq)
            o_p, s_p = _gdn_prompt(qkv_p, hp, beta_p.reshape(batch, GDN_V_HEADS, 1, seq),
                                   gc_p.reshape(batch, GDN_V_HEADS, 1, seq), gnorm_w_c[j], batch, seq)
            xp, up = _out_odd(o_p, w_out, xp, gt_p, g_ln, b_ln, scn_p, shn_p, seq, 512, 1024)
            ccp_l.append(buf_p)
            dp_l.append(s_p)
            hs = _matmul(us, w_in, n_main, SAMPLE_ROWS, 1024)
            qkv_s, buf_s, beta_s, eg_s = _conv_c_sample(hs, state_conv_c[j], conv_w_c[j], us, w_ba,
                                                        a_log_c[j], dt_bias_c[j])
            heads_col = lambda a: jnp.swapaxes(a.reshape(nseq, GDN_QK_HEADS, GDN_HEAD_DIM), 1, 2)
            q_col = heads_col(qkv_s[:nseq, 0:GDN_QK_W])
            k_col = heads_col(qkv_s[:nseq, GDN_QK_W:2 * GDN_QK_W])
            v_s = qkv_s[:nseq, 2 * GDN_QK_W:].reshape(nseq, GDN_V_HEADS, GDN_HEAD_DIM)
            z_s = hs[:nseq, GDN_CONV_DIM:n_main].reshape(nseq, GDN_V_HEADS, GDN_HEAD_DIM)
            o_s, s_s = _gdn_sample(beta_s[:nseq], eg_s[:nseq], q_col, k_col, v_s, z_s, gnorm_w_c[j], state_delta[j])
            a_s = jnp.pad(o_s.reshape(nseq, GDN_V_W), ((0, pad), (0, 0))).astype(BF16)
            xs, us = _out_odd(a_s, w_out, xs, gt_s, g_ln, b_ln, scn_s, shn_s, SAMPLE_ROWS, SAMPLE_ROWS, 1024)
            ccs_l.append(buf_s)
            ds_l.append(s_s)
    return (xp.reshape(batch, seq, d), xs[:nseq].reshape(nseq, 1, d),
            jnp.stack(kp_l), jnp.stack(vp_l), jnp.stack(ks_l), jnp.stack(vs_l),
            jnp.stack(cap_l), jnp.stack(cas_l), jnp.stack(ccp_l), jnp.stack(ccs_l),
            jnp.stack(dp_l), jnp.stack(ds_l))
```

```python
import functools

import jax
import jax.numpy as jnp
from jax import lax
from jax.experimental import pallas as pl
from jax.experimental.pallas import tpu as pltpu

F32 = jnp.float32
BF16 = jnp.bfloat16

D_MODEL = 2048
DEPTH = 4
CONV_WIDTH_A = 31
W_A = 1024
GROUP_A = 128
SB_HEAD_DIM = 128
SB_HEADS = 8
W_B = 1024
GDN_HEAD_DIM = 128
GDN_QK_HEADS = 16
GDN_V_HEADS = 32
GDN_QK_W = 2048
GDN_V_W = 4096
GDN_CONV_WIDTH = 4
GDN_CONV_DIM = 8192
GDN_CHUNK = 128
DEEPNORM_ALPHA = (2 * DEPTH) ** 0.25
LN_EPS = 1e-5
RMS_EPS = 1e-6
SAMPLE_ROWS = 16
LANES = 128
VMEM_LIMIT = 48 * 1024 * 1024

NN = (((1,), (0,)), ((), ()))
NT = (((1,), (1,)), ((), ()))


def _params(*sem):
    return pltpu.CompilerParams(dimension_semantics=sem, vmem_limit_bytes=VMEM_LIMIT)


def _dot(a, b, dims=NN):
    return lax.dot_general(a, b, dims, preferred_element_type=F32)


def _split(x):
    hi = x.astype(BF16)
    lo = (x - hi.astype(F32)).astype(BF16)
    return hi, lo


def _dot3(a, b, dims=NN):
    ah, al = _split(a)
    bh, bl = _split(b)
    return _dot(ah, bh, dims) + (_dot(ah, bl, dims) + _dot(al, bh, dims))


def _dot_exact_rhs(a, m_bf16):
    ah, al = _split(a)
    return _dot(ah, m_bf16) + _dot(al, m_bf16)


def _sigmoid(x):
    return 1.0 / (1.0 + jnp.exp(-x))


def _silu(x):
    return x * _sigmoid(x)


def _softplus(x):
    return jnp.maximum(x, 0.0) + jnp.log(1.0 + jnp.exp(-jnp.abs(x)))


def _ada_kernel(c_ref, w_ref, b_ref, o_ref):
    s = _silu(c_ref[...]).astype(BF16)
    o_ref[0] = _dot(s, w_ref[0].astype(BF16)) + b_ref[0]


def _ada_all(c_all, w_ada, b_ada):
    depth, d, n = w_ada.shape
    rows = c_all.shape[0]
    tn = 512
    return pl.pallas_call(
        _ada_kernel,
        grid=(depth, n // tn),
        in_specs=[pl.BlockSpec((rows, d), lambda l, j: (0, 0)),
                  pl.BlockSpec((1, d, tn), lambda l, j: (l, 0, j)),
                  pl.BlockSpec((1, 1, tn), lambda l, j: (l, 0, j))],
        out_specs=pl.BlockSpec((1, rows, tn), lambda l, j: (l, 0, j)),
        out_shape=jax.ShapeDtypeStruct((depth, rows, n), F32),
        compiler_params=_params("arbitrary", "arbitrary"),
        name="ada_mod",
    )(c_all, w_ada, b_ada.reshape(depth, 1, n))


def _modulate_kernel(x_ref, sc_ref, sh_ref, u_ref):
    u_ref[...] = (x_ref[...] * (1.0 + sc_ref[0]) + sh_ref[0]).astype(u_ref.dtype)


def _modulate(x, sc, sh, rows_per_mod, tm):
    m, d = x.shape
    r = sc.shape[1]
    per = rows_per_mod // tm
    return pl.pallas_call(
        _modulate_kernel,
        grid=(m // tm,),
        in_specs=[pl.BlockSpec((tm, d), lambda i: (i, 0)),
                  pl.BlockSpec((1, r, d), lambda i: (i // per, 0, 0)),
                  pl.BlockSpec((1, r, d), lambda i: (i // per, 0, 0))],
        out_specs=pl.BlockSpec((tm, d), lambda i: (i, 0)),
        out_shape=jax.ShapeDtypeStruct((m, d), BF16),
        compiler_params=_params("arbitrary"),
        name="modulate",
    )(x, sc, sh)


def _matmul_kernel(a_ref, w_ref, o_ref):
    o_ref[...] = _dot(a_ref[...], w_ref[...]).astype(o_ref.dtype)


def _matmul(a, w, n_out, tm, tn):
    m, k = a.shape
    return pl.pallas_call(
        _matmul_kernel,
        grid=(m // tm, n_out // tn),
        in_specs=[pl.BlockSpec((tm, k), lambda i, j: (i, 0)),
                  pl.BlockSpec((k, tn), lambda i, j: (0, j))],
        out_specs=pl.BlockSpec((tm, tn), lambda i, j: (i, j)),
        out_shape=jax.ShapeDtypeStruct((m, n_out), F32),
        compiler_params=_params("arbitrary", "arbitrary"),
        name="in_proj",
    )(a, w)


def _post_norm(h, x, gt, g, b):
    r = DEEPNORM_ALPHA * x + (1.0 + gt) * h
    mu = jnp.mean(r, axis=-1, keepdims=True)
    rc = r - mu
    var = jnp.mean(rc * rc, axis=-1, keepdims=True)
    return rc * lax.rsqrt(var + LN_EPS) * g + b


def _out_even_kernel(ya_ref, yb_ref, w_ref, x_ref, gt_ref, g_ref, b_ref, sc_ref, sh_ref, xo_ref, uo_ref):
    ka = ya_ref.shape[1]
    h = _dot(ya_ref[...], w_ref[0:ka, :]) + _dot(yb_ref[...], w_ref[ka:, :])
    y = _post_norm(h, x_ref[...], gt_ref[0], g_ref[...], b_ref[...])
    xo_ref[...] = y
    uo_ref[...] = (y * (1.0 + sc_ref[0]) + sh_ref[0]).astype(uo_ref.dtype)


def _out_even(ya, yb, w, x, gt, g, b, sc, sh, rows_per_mod, tm):
    m, d = x.shape
    ka, kb = ya.shape[1], yb.shape[1]
    r = gt.shape[1]
    per = rows_per_mod // tm
    mod = pl.BlockSpec((1, r, d), lambda i: (i // per, 0, 0))
    vec = pl.BlockSpec((1, d), lambda i: (0, 0))
    return pl.pallas_call(
        _out_even_kernel,
        grid=(m // tm,),
        in_specs=[pl.BlockSpec((tm, ka), lambda i: (i, 0)),
                  pl.BlockSpec((tm, kb), lambda i: (i, 0)),
                  pl.BlockSpec((ka + kb, d), lambda i: (0, 0)),
                  pl.BlockSpec((tm, d), lambda i: (i, 0)),
                  mod, vec, vec, mod, mod],
        out_specs=[pl.BlockSpec((tm, d), lambda i: (i, 0)),
                   pl.BlockSpec((tm, d), lambda i: (i, 0))],
        out_shape=[jax.ShapeDtypeStruct((m, d), F32), jax.ShapeDtypeStruct((m, d), BF16)],
        compiler_params=_params("arbitrary"),
        name="out_proj_even",
    )(ya, yb, w, x, gt, g, b, sc, sh)


def _out_odd_kernel(a_ref, w_ref, x_ref, gt_ref, g_ref, b_ref, sc_ref, sh_ref, xo_ref, uo_ref, acc_ref):
    k = pl.program_id(1)

    @pl.when(k == 0)
    def _():
        acc_ref[...] = jnp.zeros_like(acc_ref)

    acc_ref[...] += _dot(a_ref[...], w_ref[...])

    @pl.when(k == pl.num_programs(1) - 1)
    def _():
        y = _post_norm(acc_ref[...], x_ref[...], gt_ref[0], g_ref[...], b_ref[...])
        xo_ref[...] = y
        uo_ref[...] = (y * (1.0 + sc_ref[0]) + sh_ref[0]).astype(uo_ref.dtype)


def _out_odd(a, w, x, gt, g, b, sc, sh, rows_per_mod, tm, tk):
    m, d = x.shape
    kdim = a.shape[1]
    r = gt.shape[1]
    per = rows_per_mod // tm
    mod = pl.BlockSpec((1, r, d), lambda i, k: (i // per, 0, 0))
    vec = pl.BlockSpec((1, d), lambda i, k: (0, 0))
    return pl.pallas_call(
        _out_odd_kernel,
        grid=(m // tm, kdim // tk),
        in_specs=[pl.BlockSpec((tm, tk), lambda i, k: (i, k)),
                  pl.BlockSpec((tk, d), lambda i, k: (k, 0)),
                  pl.BlockSpec((tm, d), lambda i, k: (i, 0)),
                  mod, vec, vec, mod, mod],
        out_specs=[pl.BlockSpec((tm, d), lambda i, k: (i, 0)),
                   pl.BlockSpec((tm, d), lambda i, k: (i, 0))],
        out_shape=[jax.ShapeDtypeStruct((m, d), F32), jax.ShapeDtypeStruct((m, d), BF16)],
        scratch_shapes=[pltpu.VMEM((tm, d), F32)],
        compiler_params=_params("arbitrary", "arbitrary"),
        name="out_proj_odd",
    )(a, w, x, gt, g, b, sc, sh)


CONV_A_HALO = 32
CONV_A_ROWS = 64


def _group_norm_gate(y, gate, g, b):
    outs = []
    for s in range(y.shape[1] // GROUP_A):
        sl = slice(s * GROUP_A, (s + 1) * GROUP_A)
        seg = y[:, sl]
        mu = jnp.mean(seg, axis=-1, keepdims=True)
        sc = seg - mu
        var = jnp.mean(sc * sc, axis=-1, keepdims=True)
        yn = sc * lax.rsqrt(var + LN_EPS) * g[:, sl] + b[:, sl]
        outs.append(_silu(yn) * _silu(gate[:, sl]))
    return outs[0] if len(outs) == 1 else jnp.concatenate(outs, axis=1)


def _conv_a_kernel(val_ref, glu_ref, gate_ref, w_ref, g_ref, b_ref, y_ref, st_ref, ext_ref):
    t = pl.program_id(2)
    tt = val_ref.shape[0]
    kw = w_ref.shape[0]
    off = CONV_A_HALO - (kw - 1)

    @pl.when(t == 0)
    def _():
        ext_ref[0:CONV_A_HALO, :] = jnp.zeros((CONV_A_HALO, ext_ref.shape[1]), F32)

    ext_ref[CONV_A_HALO:, :] = val_ref[...] * _sigmoid(glu_ref[...])
    for r0 in range(0, tt, CONV_A_ROWS):
        acc = w_ref[0:1, :] * ext_ref[r0 + off:r0 + off + CONV_A_ROWS, :]
        for j in range(1, kw):
            acc = acc + w_ref[j:j + 1, :] * ext_ref[r0 + off + j:r0 + off + j + CONV_A_ROWS, :]
        y = _group_norm_gate(acc, gate_ref[r0:r0 + CONV_A_ROWS, :], g_ref[...], b_ref[...])
        y_ref[r0:r0 + CONV_A_ROWS, :] = y.astype(y_ref.dtype)

    @pl.when(t == pl.num_programs(2) - 1)
    def _():
        st_ref[0] = ext_ref[tt + off:tt + CONV_A_HALO, :]

    ext_ref[0:CONV_A_HALO, :] = ext_ref[tt:tt + CONV_A_HALO, :]


def _conv_a_prompt(h, conv_w, gn_g, gn_b, batch, seq, tt=256, tc=256):
    nt = seq // tt
    nc = W_A // tc
    kw = conv_w.shape[0]
    col = lambda base: pl.BlockSpec((tt, tc), lambda b, c, t: (b * nt + t, base * nc + c))
    vec = pl.BlockSpec((1, tc), lambda b, c, t: (0, c))
    return pl.pallas_call(
        _conv_a_kernel,
        grid=(batch, nc, nt),
        in_specs=[col(0), col(1), col(2),
                  pl.BlockSpec((kw, tc), lambda b, c, t: (0, c)), vec, vec],
        out_specs=[pl.BlockSpec((tt, tc), lambda b, c, t: (b * nt + t, c)),
                   pl.BlockSpec((1, kw - 1, tc), lambda b, c, t: (b, 0, c))],
        out_shape=[jax.ShapeDtypeStruct((batch * seq, W_A), BF16),
                   jax.ShapeDtypeStruct((batch, kw - 1, W_A), F32)],
        scratch_shapes=[pltpu.VMEM((CONV_A_HALO + tt, tc), F32)],
        compiler_params=_params("arbitrary", "arbitrary", "arbitrary"),
        name="conv_a_prompt",
    )(h, h, h, conv_w, gn_g.reshape(1, -1), gn_b.reshape(1, -1))


def _conv_a_sample_kernel(h_ref, st_ref, w_ref, g_ref, b_ref, y_ref, so_ref):
    nseq, kw1, _ = st_ref.shape
    glu = h_ref[:, 0:W_A] * _sigmoid(h_ref[:, W_A:2 * W_A])
    rows = []
    for s in range(nseq):
        conv = jnp.sum(st_ref[s] * w_ref[0:kw1, :], axis=0, keepdims=True) + glu[s:s + 1, :] * w_ref[kw1:kw1 + 1, :]
        rows.append(conv)
        so_ref[s, 0:kw1 - 1, :] = st_ref[s, 1:kw1, :]
        so_ref[s, kw1 - 1:kw1, :] = glu[s:s + 1, :]
    rows.append(jnp.zeros((h_ref.shape[0] - nseq, W_A), F32))
    conv = jnp.concatenate(rows, axis=0)
    y = _group_norm_gate(conv, h_ref[:, 2 * W_A:3 * W_A], g_ref[...], b_ref[...])
    y_ref[...] = y.astype(y_ref.dtype)


def _conv_a_sample(h_s, state, conv_w, gn_g, gn_b):
    rows = h_s.shape[0]
    return pl.pallas_call(
        _conv_a_sample_kernel,
        out_shape=[jax.ShapeDtypeStruct((rows, W_A), BF16), jax.ShapeDtypeStruct(state.shape, F32)],
        compiler_params=pltpu.CompilerParams(vmem_limit_bytes=VMEM_LIMIT),
        name="conv_a_sample",
    )(h_s, state, conv_w, gn_g.reshape(1, -1), gn_b.reshape(1, -1))


def _attn_prompt_kernel(bias_ref, q_ref, k_ref, v_ref, gate_ref, o_ref, acc_ref, later_ref):
    h = pl.program_id(1)
    i = pl.program_id(2)
    blk = q_ref.shape[0]
    bias = bias_ref[h]
    q = (q_ref[...] * (SB_HEAD_DIM ** -0.5)).astype(BF16)
    row = lax.broadcasted_iota(jnp.int32, (blk, blk), 0)
    col = lax.broadcasted_iota(jnp.int32, (blk, blk), 1)
    after = jnp.where(row > col, 1.0, 0.0).astype(BF16)
    visible = col < row

    acc_ref[...] = jnp.zeros_like(acc_ref)
    later_ref[...] = jnp.zeros_like(later_ref)

    def blocks(js, masked):
        zs, sps = [], []
        for j in js:
            start = pl.multiple_of(j * blk, blk)
            z = _dot(q, k_ref[pl.ds(start, blk), :].astype(BF16), NT) + bias
            sp = _softplus(z)
            if masked:
                sp = jnp.where(visible, sp, 0.0)
            zs.append(z)
            sps.append(sp)
        later = later_ref[...]
        out = None
        for j, z, sp in zip(js, zs, sps):
            start = pl.multiple_of(j * blk, blk)
            hi, lo = _split(sp)
            both = _dot(jnp.concatenate([hi, lo], axis=0), after)
            inside = both[0:blk, :] + both[blk:, :]
            w = jnp.exp(z - sp - inside - later)
            if masked:
                w = jnp.where(visible, w, 0.0)
            pv = _dot(w.astype(BF16), v_ref[pl.ds(start, blk), :].astype(BF16))
            out = pv if out is None else out + pv
            later = later + jnp.sum(sp, axis=-1, keepdims=True)
        acc_ref[...] += out
        later_ref[...] = later

    blocks([i], True)

    def body(jj, carry):
        j = i - 1 - 2 * jj
        blocks([j, j - 1], False)
        return carry

    lax.fori_loop(0, i // 2, body, 0)

    @pl.when(i % 2 == 1)
    def _():
        blocks([0], False)

    o_ref[...] = (acc_ref[...] * _silu(gate_ref[...])).astype(o_ref.dtype)


def _attn_prompt(h, sb_bias, batch, seq, blk=256):
    nq = seq // blk
    base = 3 * W_A // SB_HEAD_DIM
    return pl.pallas_call(
        _attn_prompt_kernel,
        grid_spec=pltpu.PrefetchScalarGridSpec(
            num_scalar_prefetch=0,
            grid=(batch, SB_HEADS, nq),
            in_specs=[pl.BlockSpec(memory_space=pltpu.SMEM),
                      pl.BlockSpec((blk, SB_HEAD_DIM), lambda b, hh, i: (b * nq + i, base + hh)),
                      pl.BlockSpec((seq, SB_HEAD_DIM), lambda b, hh, i: (b, base + SB_HEADS + hh)),
                      pl.BlockSpec((seq, SB_HEAD_DIM), lambda b, hh, i: (b, base + 2 * SB_HEADS + hh)),
                      pl.BlockSpec((blk, SB_HEAD_DIM), lambda b, hh, i: (b * nq + i, base + 3 * SB_HEADS + hh))],
            out_specs=pl.BlockSpec((blk, SB_HEAD_DIM), lambda b, hh, i: (b * nq + i, hh)),
            scratch_shapes=[pltpu.VMEM((blk, SB_HEAD_DIM), F32), pltpu.VMEM((blk, 1), F32)]),
        out_shape=jax.ShapeDtypeStruct((batch * seq, W_B), BF16),
        compiler_params=_params("arbitrary", "arbitrary", "arbitrary"),
        name="attn_prompt",
    )(sb_bias, h, h, h, h)


PAGES_PER_STEP = 4


def _attn_sample_kernel(pt_ref, q_ref, bias_ref, gate_ref, *refs):
    npg = PAGES_PER_STEP
    k_refs, v_refs = refs[:npg], refs[npg:2 * npg]
    o_ref, acc_ref, later_ref = refs[2 * npg:]
    p = pl.program_id(1)
    rows = q_ref.shape[1]
    page = k_refs[0].shape[2] // SB_HEADS

    @pl.when(p == 0)
    def _():
        acc_ref[...] = jnp.zeros_like(acc_ref)
        later_ref[...] = jnp.zeros_like(later_ref)

    q = (q_ref[0] * (SB_HEAD_DIM ** -0.5)).astype(BF16)
    rid = lax.broadcasted_iota(jnp.int32, (rows, page), 0)
    r2 = lax.broadcasted_iota(jnp.int32, (page, page), 0)
    c2 = lax.broadcasted_iota(jnp.int32, (page, page), 1)
    after = jnp.where(r2 > c2, 1.0, 0.0).astype(BF16)
    for r in range(npg):
        z = jnp.zeros((rows, page), F32)
        for hh in range(SB_HEADS):
            kh = k_refs[r][0, 0, pl.ds(hh, page, stride=SB_HEADS), :].astype(BF16)
            z = jnp.where(rid == hh, _dot(q, kh, NT), z)
        z = z + bias_ref[...]
        sp = _softplus(z)
        inside = _dot_exact_rhs(sp, after)
        w = jnp.exp(z - sp - inside - later_ref[...]).astype(BF16)
        o = jnp.zeros((rows, SB_HEAD_DIM), F32)
        for hh in range(SB_HEADS):
            vh = v_refs[r][0, 0, pl.ds(hh, page, stride=SB_HEADS), :].astype(BF16)
            o = jnp.where(rid == hh, _dot(w, vh), o)
        acc_ref[...] += o
        later_ref[...] += jnp.sum(sp, axis=-1, keepdims=True)

    @pl.when(p == pl.num_programs(1) - 1)
    def _():
        o_ref[0] = acc_ref[...] * _silu(gate_ref[0])


def _attn_sample(q, gate, bias, cache_k, cache_v, page_table, layer):
    nseq, rows, _ = q.shape
    n_pages = page_table.shape[1]
    npg = PAGES_PER_STEP
    blk = cache_k.shape[2]

    def page_spec(r):
        return pl.BlockSpec((1, 1, blk, SB_HEAD_DIM),
                            lambda s, p, pt: (layer, pt[s, n_pages - 1 - (p * npg + r)], 0, 0))

    row_spec = pl.BlockSpec((1, rows, SB_HEAD_DIM), lambda s, p, pt: (s, 0, 0))
    return pl.pallas_call(
        _attn_sample_kernel,
        grid_spec=pltpu.PrefetchScalarGridSpec(
            num_scalar_prefetch=1,
            grid=(nseq, n_pages // npg),
            in_specs=[row_spec, pl.BlockSpec((rows, 1), lambda s, p, pt: (0, 0)), row_spec]
                     + [page_spec(r) for r in range(npg)] + [page_spec(r) for r in range(npg)],
            out_specs=row_spec,
            scratch_shapes=[pltpu.VMEM((rows, SB_HEAD_DIM), F32), pltpu.VMEM((rows, 1), F32)]),
        out_shape=jax.ShapeDtypeStruct((nseq, rows, SB_HEAD_DIM), F32),
        compiler_params=_params("arbitrary", "arbitrary"),
        name="attn_sample",
    )(page_table, q, bias, gate, *([cache_k] * npg), *([cache_v] * npg))


def _gdn_gate_kernel(u_ref, wt_ref, alog_ref, dt_ref, beta_ref, gc_ref):
    raw = _dot(wt_ref[...], u_ref[...], NT)
    hv = raw.shape[0] // 2
    beta_ref[0] = _sigmoid(raw[0:hv, :])
    g = -jnp.exp(alog_ref[...]) * _softplus(raw[hv:, :] + dt_ref[...])
    r2 = lax.broadcasted_iota(jnp.int32, (GDN_CHUNK, GDN_CHUNK), 0)
    c2 = lax.broadcasted_iota(jnp.int32, (GDN_CHUNK, GDN_CHUNK), 1)
    upto = jnp.where(r2 <= c2, 1.0, 0.0).astype(BF16)
    gh, gl = _split(g)
    gl2 = (g - gh.astype(F32) - gl.astype(F32)).astype(BF16)
    for c in range(g.shape[1] // GDN_CHUNK):
        sl = slice(c * GDN_CHUNK, (c + 1) * GDN_CHUNK)
        gc_ref[0, :, sl] = _dot(gh[:, sl], upto) + (_dot(gl[:, sl], upto) + _dot(gl2[:, sl], upto))


def _gdn_gates(u, w_ba_t, a_log, dt_bias, batch, seq, tt=512):
    nt = seq // tt
    hv2, d = w_ba_t.shape
    hv = hv2 // 2
    out = pl.BlockSpec((1, hv, tt), lambda b, t: (b, 0, t))
    colv = pl.BlockSpec((hv, 1), lambda b, t: (0, 0))
    return pl.pallas_call(
        _gdn_gate_kernel,
        grid=(batch, nt),
        in_specs=[pl.BlockSpec((tt, d), lambda b, t: (b * nt + t, 0)),
                  pl.BlockSpec((hv2, d), lambda b, t: (0, 0)), colv, colv],
        out_specs=[out, out],
        out_shape=[jax.ShapeDtypeStruct((batch, hv, seq), F32)] * 2,
        compiler_params=_params("arbitrary", "arbitrary"),
        name="gdn_gates",
    )(u, w_ba_t, a_log.reshape(hv, 1), dt_bias.reshape(hv, 1))


CONV_C_HALO = 8


def _qk_norm(y, cblk, tc):
    first = cblk * tc
    is_q = first < GDN_QK_W
    is_qk = first < 2 * GDN_QK_W
    outs = []
    for s in range(tc // GDN_HEAD_DIM):
        seg = y[:, s * GDN_HEAD_DIM:(s + 1) * GDN_HEAD_DIM]
        rs = lax.rsqrt(jnp.sum(seg * seg, axis=-1, keepdims=True) + RMS_EPS)
        f = jnp.where(is_q, rs * (GDN_HEAD_DIM ** -0.5), jnp.where(is_qk, rs, 1.0))
        outs.append(seg * f)
    return outs[0] if len(outs) == 1 else jnp.concatenate(outs, axis=1)


def _conv_c_kernel(x_ref, w_ref, y_ref, st_ref, ext_ref):
    c = pl.program_id(1)
    t = pl.program_id(2)
    tt, tc = x_ref.shape
    kw = w_ref.shape[0]
    off = CONV_C_HALO - (kw - 1)

    @pl.when(t == 0)
    def _():
        ext_ref[0:CONV_C_HALO, :] = jnp.zeros((CONV_C_HALO, tc), F32)

    ext_ref[CONV_C_HALO:, :] = x_ref[...]
    acc = w_ref[0:1, :] * ext_ref[off:off + tt, :]
    for j in range(1, kw):
        acc = acc + w_ref[j:j + 1, :] * ext_ref[off + j:off + j + tt, :]
    y_ref[...] = _qk_norm(_silu(acc), c, tc)

    @pl.when(t == pl.num_programs(2) - 1)
    def _():
        st_ref[0] = ext_ref[tt + off:tt + CONV_C_HALO, :]

    ext_ref[0:CONV_C_HALO, :] = ext_ref[tt:tt + CONV_C_HALO, :]


def _conv_c_prompt(h, conv_w, batch, seq, tt=256, tc=512):
    nt = seq // tt
    nc = GDN_CONV_DIM // tc
    kw = conv_w.shape[0]
    return pl.pallas_call(
        _conv_c_kernel,
        grid=(batch, nc, nt),
        in_specs=[pl.BlockSpec((tt, tc), lambda b, c, t: (b * nt + t, c)),
                  pl.BlockSpec((kw, tc), lambda b, c, t: (0, c))],
        out_specs=[pl.BlockSpec((tt, tc), lambda b, c, t: (b * nt + t, c)),
                   pl.BlockSpec((1, kw - 1, tc), lambda b, c, t: (b, 0, c))],
        out_shape=[jax.ShapeDtypeStruct((batch * seq, GDN_CONV_DIM), F32),
                   jax.ShapeDtypeStruct((batch, kw - 1, GDN_CONV_DIM), F32)],
        scratch_shapes=[pltpu.VMEM((CONV_C_HALO + tt, tc), F32)],
        compiler_params=_params("arbitrary", "arbitrary", "arbitrary"),
        name="conv_c_prompt",
    )(h, conv_w)


def _gdn_prompt_kernel(q_ref, k_ref, v_ref, z_ref, beta_ref, gc_ref, gw_ref, o_ref, s_out_ref, s_ref):
    t = pl.program_id(2)
    c = GDN_CHUNK
    tt = q_ref.shape[0]
    rep = v_ref.shape[1] // GDN_HEAD_DIM

    @pl.when(t == 0)
    def _():
        s_ref[...] = jnp.zeros_like(s_ref)

    row = lax.broadcasted_iota(jnp.int32, (c, c), 0)
    col = lax.broadcasted_iota(jnp.int32, (c, c), 1)
    incl = row >= col
    strict = row > col
    eye = jnp.where(row == col, 1.0, 0.0)
    merge = []
    m = 1
    while m < c:
        bi, bj = row // m, col // m
        merge.append(jnp.logical_and(bi - bj == 1, bj % 2 == 0))
        m *= 2

    bf = lambda x: x.astype(BF16)
    nchunk = tt // c
    units = [(n, hv) for n in range(nchunk) for hv in range(rep)]
    rows_of = lambda n: slice(n * c, (n + 1) * c)
    cols_of = lambda hv: slice(hv * GDN_HEAD_DIM, (hv + 1) * GDN_HEAD_DIM)

    kq = {}
    for n in range(nchunk):
        kb = bf(k_ref[rows_of(n), :])
        kq[n] = _dot(jnp.concatenate([kb, bf(q_ref[rows_of(n), :])], axis=0), kb, NT)
    g_row, b_row, g_col, b_col, decay, lmat, xo = {}, {}, {}, {}, {}, {}, {}
    for un in units:
        n, hv = un
        g_row[un] = jnp.broadcast_to(gc_ref[0, hv, :, rows_of(n)], (c, c))
        b_row[un] = jnp.broadcast_to(beta_ref[0, hv, :, rows_of(n)], (c, c))
        g_col[un] = g_row[un].T
        b_col[un] = b_row[un].T
        decay[un] = jnp.where(incl, jnp.exp(jnp.where(incl, g_col[un] - g_row[un], 0.0)), 0.0)
        lmat[un] = jnp.where(strict, kq[n][0:c, :] * b_col[un] * decay[un], 0.0)
        xo[un] = -jnp.where(merge[0], lmat[un], 0.0)
    for lvl in merge[1:]:
        for un in units:
            cm = jnp.where(lvl, lmat[un], 0.0)
            y = cm + _dot(bf(xo[un]), bf(cm))
            xo[un] = xo[un] - (y + _dot(bf(y), bf(xo[un])))
    u_val, wq, intra, k_dec_t, g_last = {}, {}, {}, {}, {}
    for un in units:
        n, hv = un
        k = k_ref[rows_of(n), :]
        v = v_ref[rows_of(n), cols_of(hv)]
        xb = xo[un] * b_row[un]
        u_val[un] = v * b_col[un] + _dot(bf(xb), bf(v))
        eg_col = jnp.exp(g_col[un])
        w_key = k * (b_col[un] * eg_col) + _dot(bf(xb * jnp.exp(g_row[un])), bf(k))
        wq[un] = bf(jnp.concatenate([w_key, q_ref[rows_of(n), :] * eg_col], axis=0))
        intra[un] = bf(jnp.where(incl, kq[n][c:, :] * decay[un], 0.0))
        g_last[un] = g_col[un][c - 1:c, :]
        k_dec_t[un] = bf((k * jnp.exp(g_last[un] - g_col[un])).T)

    for un in units:
        n, hv = un
        s = s_ref[hv]
        ws = _dot(wq[un], bf(s))
        v_new = bf(u_val[un] - ws[0:c, :])
        o = ws[c:, :] + _dot(intra[un], v_new)
        s_ref[hv] = s * jnp.exp(g_last[un]) + _dot(k_dec_t[un], v_new)
        o = o * lax.rsqrt(jnp.mean(o * o, axis=-1, keepdims=True) + RMS_EPS) * gw_ref[...]
        o_ref[rows_of(n), cols_of(hv)] = (o * _silu(z_ref[rows_of(n), cols_of(hv)])).astype(o_ref.dtype)

    @pl.when(t == pl.num_programs(2) - 1)
    def _():
        s_out_ref[0] = s_ref[...]


def _gdn_prompt(qkv, h, beta, gc, gnorm_w, batch, seq, tt=512):
    nt = seq // tt
    rep = GDN_V_HEADS // GDN_QK_HEADS
    dh = GDN_HEAD_DIM
    vw = rep * dh
    gate = pl.BlockSpec((1, rep, 1, tt), lambda b, hq, t: (b, hq, 0, t))
    return pl.pallas_call(
        _gdn_prompt_kernel,
        grid=(batch, GDN_QK_HEADS, nt),
        in_specs=[pl.BlockSpec((tt, dh), lambda b, hq, t: (b * nt + t, hq)),
                  pl.BlockSpec((tt, dh), lambda b, hq, t: (b * nt + t, GDN_QK_HEADS + hq)),
                  pl.BlockSpec((tt, vw), lambda b, hq, t: (b * nt + t, 2 * GDN_QK_W // vw + hq)),
                  pl.BlockSpec((tt, vw), lambda b, hq, t: (b * nt + t, GDN_CONV_DIM // vw + hq)),
                  gate, gate,
                  pl.BlockSpec((1, dh), lambda b, hq, t: (0, 0))],
        out_specs=[pl.BlockSpec((tt, vw), lambda b, hq, t: (b * nt + t, hq)),
                   pl.BlockSpec((1, rep, dh, dh), lambda b, hq, t: (b, hq, 0, 0))],
        out_shape=[jax.ShapeDtypeStruct((batch * seq, GDN_V_W), BF16),
                   jax.ShapeDtypeStruct((batch, GDN_V_HEADS, dh, dh), F32)],
        scratch_shapes=[pltpu.VMEM((rep, dh, dh), F32)],
        compiler_params=_params("arbitrary", "arbitrary", "arbitrary"),
        name="gdn_prompt",
    )(qkv, qkv, qkv, h, beta, gc, gnorm_w.reshape(1, dh))


def _conv_c_sample_kernel(h_ref, st_ref, w_ref, u_ref, wba_ref, alog_ref, dt_ref, y_ref, so_ref, beta_ref, eg_ref):
    nseq, kw1, width = st_ref.shape
    tc = 1024
    for c0 in range(0, width, tc):
        cs = slice(c0, c0 + tc)
        rows = []
        for s in range(nseq):
            x_new = h_ref[s:s + 1, cs]
            conv = jnp.sum(st_ref[s, :, cs] * w_ref[0:kw1, cs], axis=0, keepdims=True) + x_new * w_ref[kw1:kw1 + 1, cs]
            rows.append(conv)
            so_ref[s, 0:kw1 - 1, cs] = st_ref[s, 1:kw1, cs]
            so_ref[s, kw1 - 1:kw1, cs] = x_new
        rows.append(jnp.zeros((h_ref.shape[0] - nseq, tc), F32))
        y_ref[:, cs] = _qk_norm(_silu(jnp.concatenate(rows, axis=0)), c0 // tc, tc)
    raw = _dot(u_ref[...], wba_ref[...])
    hv = raw.shape[1] // 2
    beta_ref[...] = _sigmoid(raw[:, 0:hv])
    eg_ref[...] = jnp.exp(-jnp.exp(alog_ref[...]) * _softplus(raw[:, hv:] + dt_ref[...]))


def _conv_c_sample(h_s, state, conv_w, u_s, w_ba, a_log, dt_bias):
    rows = h_s.shape[0]
    hv = a_log.shape[0]
    return pl.pallas_call(
        _conv_c_sample_kernel,
        out_shape=[jax.ShapeDtypeStruct((rows, GDN_CONV_DIM), F32), jax.ShapeDtypeStruct(state.shape, F32),
                   jax.ShapeDtypeStruct((rows, hv), F32), jax.ShapeDtypeStruct((rows, hv), F32)],
        compiler_params=pltpu.CompilerParams(vmem_limit_bytes=VMEM_LIMIT),
        name="conv_c_sample",
    )(h_s, state, conv_w, u_s, w_ba, a_log.reshape(1, hv), dt_bias.reshape(1, hv))


def _gdn_sample_kernel(beta_ref, eg_ref, qc_ref, kc_ref, v_ref, z_ref, gw_ref, s_ref, o_ref, so_ref):
    sq = pl.program_id(0)
    rep = GDN_V_HEADS // GDN_QK_HEADS
    for hv in range(GDN_V_HEADS):
        hq = hv // rep
        k_col = kc_ref[0, :, hq:hq + 1]
        q_col = qc_ref[0, :, hq:hq + 1]
        s = s_ref[0, hv] * eg_ref[sq, hv]
        delta = (v_ref[0, hv:hv + 1, :] - jnp.sum(s * k_col, axis=0, keepdims=True)) * beta_ref[sq, hv]
        s = s + k_col * delta
        so_ref[0, hv] = s
        o = jnp.sum(s * q_col, axis=0, keepdims=True)
        o = o * lax.rsqrt(jnp.mean(o * o, axis=-1, keepdims=True) + RMS_EPS) * gw_ref[...]
        o_ref[0, hv:hv + 1, :] = o * _silu(z_ref[0, hv:hv + 1, :])


def _gdn_sample(beta, eg, q_col, k_col, v, z, gnorm_w, state):
    nseq, hv, dk, dv = state.shape
    hq = q_col.shape[2]
    smem = pl.BlockSpec(memory_space=pltpu.SMEM)
    colspec = pl.BlockSpec((1, dk, hq), lambda s: (s, 0, 0))
    rowspec = pl.BlockSpec((1, hv, dv), lambda s: (s, 0, 0))
    stspec = pl.BlockSpec((1, hv, dk, dv), lambda s: (s, 0, 0, 0))
    return pl.pallas_call(
        _gdn_sample_kernel,
        grid=(nseq,),
        in_specs=[smem, smem, colspec, colspec, rowspec, rowspec,
                  pl.BlockSpec((1, dv), lambda s: (0, 0)), stspec],
        out_specs=[rowspec, stspec],
        out_shape=[jax.ShapeDtypeStruct((nseq, hv, dv), F32), jax.ShapeDtypeStruct(state.shape, F32)],
        compiler_params=_params("arbitrary"),
        name="gdn_sample",
    )(beta, eg, q_col, k_col, v, z, gnorm_w.reshape(1, dv), state)


def kernel(x_prompt, x_sample, c_prompt, c_sample, cache_k, cache_v, page_table, state_conv_a, state_conv_c, state_delta, w_ada, b_ada, ln_g, ln_b, w_in_even, conv_w_a, gn_g_a, gn_b_a, sb_bias, w_out_even, w_in_odd, conv_w_c, a_log_c, dt_bias_c, gnorm_w_c, w_out_odd):
    batch, seq, d = x_prompt.shape
    nseq = x_sample.shape[0]
    n_even, n_pool, page_size = cache_k.shape[:3]
    np_rows = batch * seq
    pad = SAMPLE_ROWS - nseq

    c_all = jnp.concatenate([c_prompt, c_sample, jnp.zeros((SAMPLE_ROWS - batch - nseq, d), F32)], axis=0)
    mod = _ada_all(c_all, w_ada, b_ada)

    def mods(layer):
        m = mod[layer]
        parts = [m[:, i * d:(i + 1) * d] for i in range(3)]
        pm = [p[0:batch].reshape(batch, 1, d) for p in parts]
        sm = [jnp.pad(p[batch:batch + nseq], ((0, pad), (0, 0))).reshape(1, SAMPLE_ROWS, d) for p in parts]
        return pm, sm

    xp = x_prompt.reshape(np_rows, d)
    xs = jnp.pad(x_sample.reshape(nseq, d), ((0, pad), (0, 0)))
    (sh_p, sc_p, _), (sh_s, sc_s, _) = mods(0)
    up = _modulate(xp, sc_p, sh_p, seq, 512)
    us = _modulate(xs, sc_s, sh_s, SAMPLE_ROWS, SAMPLE_ROWS)

    ck = cache_k.reshape(n_even, n_pool, page_size * SB_HEADS, SB_HEAD_DIM)
    cv = cache_v.reshape(n_even, n_pool, page_size * SB_HEADS, SB_HEAD_DIM)
    bias_pad = lambda b: jnp.pad(b, (0, SAMPLE_ROWS - SB_HEADS)).reshape(SAMPLE_ROWS, 1)
    heads_pad = lambda a: jnp.pad(a.reshape(nseq, SB_HEADS, SB_HEAD_DIM), ((0, 0), (0, SAMPLE_ROWS - SB_HEADS), (0, 0)))

    kp_l, vp_l, ks_l, vs_l = [], [], [], []
    cap_l, cas_l, ccp_l, ccs_l, dp_l, ds_l = [], [], [], [], [], []
    for layer in range(DEPTH):
        j = layer // 2
        (_, _, gt_p), (_, _, gt_s) = mods(layer)
        if layer + 1 < DEPTH:
            (shn_p, scn_p, _), (shn_s, scn_s, _) = mods(layer + 1)
        else:
            shn_p = scn_p = jnp.zeros((batch, 1, d), F32)
            shn_s = scn_s = jnp.zeros((1, SAMPLE_ROWS, d), F32)
        g_ln = ln_g[layer].reshape(1, d)
        b_ln = ln_b[layer].reshape(1, d)
        if layer % 2 == 0:
            w_in = w_in_even[j].astype(BF16)
            w_out = w_out_even[j].astype(BF16)
            n_in = w_in.shape[1]
            hp = _matmul(up, w_in, n_in, min(1024, np_rows), 1024)
            ya_p, buf_p = _conv_a_prompt(hp, conv_w_a[j], gn_g_a[j], gn_b_a[j], batch, seq)
            yb_p = _attn_prompt(hp, sb_bias[j], batch, seq)
            xp, up = _out_even(ya_p, yb_p, w_out, xp, gt_p, g_ln, b_ln, scn_p, shn_p, seq, 256)
            kp_l.append(hp[:, 3 * W_A + W_B:3 * W_A + 2 * W_B].reshape(batch, seq, SB_HEADS, SB_HEAD_DIM))
            vp_l.append(hp[:, 3 * W_A + 2 * W_B:3 * W_A + 3 * W_B].reshape(batch, seq, SB_HEADS, SB_HEAD_DIM))
            cap_l.append(buf_p)
            hs = _matmul(us, w_in, n_in, SAMPLE_ROWS, 1024)
            ya_s, buf_s = _conv_a_sample(hs, state_conv_a[j], conv_w_a[j], gn_g_a[j], gn_b_a[j])
            q_s = heads_pad(hs[:nseq, 3 * W_A:3 * W_A + W_B])
            gate_s = heads_pad(hs[:nseq, 3 * W_A + 3 * W_B:3 * W_A + 4 * W_B])
            o_s = _attn_sample(q_s, gate_s, bias_pad(sb_bias[j]), ck, cv, page_table, j)
            yb_s = jnp.pad(o_s[:, :SB_HEADS].reshape(nseq, W_B), ((0, pad), (0, 0))).astype(BF16)
            xs, us = _out_even(ya_s, yb_s, w_out, xs, gt_s, g_ln, b_ln, scn_s, shn_s, SAMPLE_ROWS, SAMPLE_ROWS)
            ks_l.append(hs[:nseq, 3 * W_A + W_B:3 * W_A + 2 * W_B].reshape(nseq, 1, SB_HEADS, SB_HEAD_DIM))
            vs_l.append(hs[:nseq, 3 * W_A + 2 * W_B:3 * W_A + 3 * W_B].reshape(nseq, 1, SB_HEADS, SB_HEAD_DIM))
            cas_l.append(buf_s)
        else:
            w_in = w_in_odd[j].astype(BF16)
            w_out = w_out_odd[j].astype(BF16)
            n_main = GDN_CONV_DIM + GDN_V_W
            w_ba = w_in[:, n_main:]
            hp = _matmul(up, w_in, n_main, min(1024, np_rows), 1024)
            beta_p, gc_p = _gdn_gates(up, w_ba.T, a_log_c[j], dt_bias_c[j], batch, seq)
            qkv_p, buf_p = _conv_c_prompt(hp, conv_w_c[j], batch, seq)
            o_p, s_p = _gdn_prompt(qkv_p, hp, beta_p.reshape(batch, GDN_V_HEADS, 1, seq),
                                   gc_p.reshape(batch, GDN_V_HEADS, 1, seq), gnorm_w_c[j], batch, seq)
            xp, up = _out_odd(o_p, w_out, xp, gt_p, g_ln, b_ln, scn_p, shn_p, seq, 512, 1024)
            ccp_l.append(buf_p)
            dp_l.append(s_p)
            hs = _matmul(us, w_in, n_main, SAMPLE_ROWS, 1024)
            qkv_s, buf_s, beta_s, eg_s = _conv_c_sample(hs, state_conv_c[j], conv_w_c[j], us, w_ba,
                                                        a_log_c[j], dt_bias_c[j])
            heads_col = lambda a: jnp.swapaxes(a.reshape(nseq, GDN_QK_HEADS, GDN_HEAD_DIM), 1, 2)
            q_col = heads_col(qkv_s[:nseq, 0:GDN_QK_W])
            k_col = heads_col(qkv_s[:nseq, GDN_QK_W:2 * GDN_QK_W])
            v_s = qkv_s[:nseq, 2 * GDN_QK_W:].reshape(nseq, GDN_V_HEADS, GDN_HEAD_DIM)
            z_s = hs[:nseq, GDN_CONV_DIM:n_main].reshape(nseq, GDN_V_HEADS, GDN_HEAD_DIM)
            o_s, s_s = _gdn_sample(beta_s[:nseq], eg_s[:nseq], q_col, k_col, v_s, z_s, gnorm_w_c[j], state_delta[j])
            a_s = jnp.pad(o_s.reshape(nseq, GDN_V_W), ((0, pad), (0, 0))).astype(BF16)
            xs, us = _out_odd(a_s, w_out, xs, gt_s, g_ln, b_ln, scn_s, shn_s, SAMPLE_ROWS, SAMPLE_ROWS, 1024)
            ccs_l.append(buf_s)
            ds_l.append(s_s)
    return (xp.reshape(batch, seq, d), xs[:nseq].reshape(nseq, 1, d),
            jnp.stack(kp_l), jnp.stack(vp_l), jnp.stack(ks_l), jnp.stack(vs_l),
            jnp.stack(cap_l), jnp.stack(cas_l), jnp.stack(ccp_l), jnp.stack(ccs_l),
            jnp.stack(dp_l), jnp.stack(ds_l))
```

```python
import functools

import jax
import jax.numpy as jnp
from jax import lax
from jax.experimental import pallas as pl
from jax.experimental.pallas import tpu as pltpu

F32 = jnp.float32
BF16 = jnp.bfloat16

D_MODEL = 2048
DEPTH = 4
CONV_WIDTH_A = 31
W_A = 1024
GROUP_A = 128
SB_HEAD_DIM = 128
SB_HEADS = 8
W_B = 1024
GDN_HEAD_DIM = 128
GDN_QK_HEADS = 16
GDN_V_HEADS = 32
GDN_QK_W = 2048
GDN_V_W = 4096
GDN_CONV_WIDTH = 4
GDN_CONV_DIM = 8192
GDN_CHUNK = 128
DEEPNORM_ALPHA = (2 * DEPTH) ** 0.25
LN_EPS = 1e-5
RMS_EPS = 1e-6
SAMPLE_ROWS = 16
LANES = 128
VMEM_LIMIT = 48 * 1024 * 1024

NN = (((1,), (0,)), ((), ()))
NT = (((1,), (1,)), ((), ()))


def _params(*sem):
    return pltpu.CompilerParams(dimension_semantics=sem, vmem_limit_bytes=VMEM_LIMIT)


def _dot(a, b, dims=NN):
    return lax.dot_general(a, b, dims, preferred_element_type=F32)


def _split(x):
    hi = x.astype(BF16)
    lo = (x - hi.astype(F32)).astype(BF16)
    return hi, lo


def _dot3(a, b, dims=NN):
    ah, al = _split(a)
    bh, bl = _split(b)
    return _dot(ah, bh, dims) + (_dot(ah, bl, dims) + _dot(al, bh, dims))


def _dot_exact_rhs(a, m_bf16):
    ah, al = _split(a)
    return _dot(ah, m_bf16) + _dot(al, m_bf16)


def _sigmoid(x):
    return 1.0 / (1.0 + jnp.exp(-x))


def _silu(x):
    return x * _sigmoid(x)


def _softplus(x):
    return jnp.maximum(x, 0.0) + jnp.log(1.0 + jnp.exp(-jnp.abs(x)))


def _ada_kernel(c_ref, w_ref, b_ref, o_ref):
    s = _silu(c_ref[...]).astype(BF16)
    o_ref[0] = _dot(s, w_ref[0].astype(BF16)) + b_ref[0]


def _ada_all(c_all, w_ada, b_ada):
    depth, d, n = w_ada.shape
    rows = c_all.shape[0]
    tn = 512
    return pl.pallas_call(
        _ada_kernel,
        grid=(depth, n // tn),
        in_specs=[pl.BlockSpec((rows, d), lambda l, j: (0, 0)),
                  pl.BlockSpec((1, d, tn), lambda l, j: (l, 0, j)),
                  pl.BlockSpec((1, 1, tn), lambda l, j: (l, 0, j))],
        out_specs=pl.BlockSpec((1, rows, tn), lambda l, j: (l, 0, j)),
        out_shape=jax.ShapeDtypeStruct((depth, rows, n), F32),
        compiler_params=_params("arbitrary", "arbitrary"),
        name="ada_mod",
    )(c_all, w_ada, b_ada.reshape(depth, 1, n))


def _modulate_kernel(x_ref, sc_ref, sh_ref, u_ref):
    u_ref[...] = (x_ref[...] * (1.0 + sc_ref[0]) + sh_ref[0]).astype(u_ref.dtype)


def _modulate(x, sc, sh, rows_per_mod, tm):
    m, d = x.shape
    r = sc.shape[1]
    per = rows_per_mod // tm
    return pl.pallas_call(
        _modulate_kernel,
        grid=(m // tm,),
        in_specs=[pl.BlockSpec((tm, d), lambda i: (i, 0)),
                  pl.BlockSpec((1, r, d), lambda i: (i // per, 0, 0)),
                  pl.BlockSpec((1, r, d), lambda i: (i // per, 0, 0))],
        out_specs=pl.BlockSpec((tm, d), lambda i: (i, 0)),
        out_shape=jax.ShapeDtypeStruct((m, d), BF16),
        compiler_params=_params("arbitrary"),
        name="modulate",
    )(x, sc, sh)


def _in_proj_kernel(a_ref, as_ref, w_ref, o_ref, os_ref, wb_ref):
    @pl.when(pl.program_id(1) == 0)
    def _():
        wb_ref[...] = w_ref[0].astype(BF16)
        os_ref[...] = _dot(as_ref[...], wb_ref[...])

    o_ref[...] = _dot(a_ref[...], wb_ref[...])


def _in_proj(a, a_s, w, layer, n_out, tm, tn):
    m, k = a.shape
    r = a_s.shape[0]
    assert m % tm == 0 and n_out % tn == 0
    return pl.pallas_call(
        _in_proj_kernel,
        grid=(n_out // tn, m // tm),
        in_specs=[pl.BlockSpec((tm, k), lambda j, i: (i, 0)),
                  pl.BlockSpec((r, k), lambda j, i: (0, 0)),
                  pl.BlockSpec((1, k, tn), lambda j, i: (layer, 0, j))],
        out_specs=[pl.BlockSpec((tm, tn), lambda j, i: (i, j)),
                   pl.BlockSpec((r, tn), lambda j, i: (0, j))],
        out_shape=[jax.ShapeDtypeStruct((m, n_out), F32), jax.ShapeDtypeStruct((r, n_out), F32)],
        scratch_shapes=[pltpu.VMEM((k, tn), BF16)],
        compiler_params=_params("arbitrary", "arbitrary"),
        name="in_proj",
    )(a, a_s, w)


def _post_norm(h, x, gt, g, b):
    r = DEEPNORM_ALPHA * x + (1.0 + gt) * h
    mu = jnp.mean(r, axis=-1, keepdims=True)
    rc = r - mu
    var = jnp.mean(rc * rc, axis=-1, keepdims=True)
    return rc * lax.rsqrt(var + LN_EPS) * g + b


def _out_even_kernel(ya_ref, yb_ref, w_ref, x_ref, gt_ref, g_ref, b_ref, sc_ref, sh_ref, xo_ref, uo_ref):
    ka = ya_ref.shape[1]
    h = _dot(ya_ref[...], w_ref[0:ka, :]) + _dot(yb_ref[...], w_ref[ka:, :])
    y = _post_norm(h, x_ref[...], gt_ref[0], g_ref[...], b_ref[...])
    xo_ref[...] = y
    uo_ref[...] = (y * (1.0 + sc_ref[0]) + sh_ref[0]).astype(uo_ref.dtype)


def _out_even(ya, yb, w, x, gt, g, b, sc, sh, rows_per_mod, tm):
    m, d = x.shape
    ka, kb = ya.shape[1], yb.shape[1]
    r = gt.shape[1]
    per = rows_per_mod // tm
    mod = pl.BlockSpec((1, r, d), lambda i: (i // per, 0, 0))
    vec = pl.BlockSpec((1, d), lambda i: (0, 0))
    return pl.pallas_call(
        _out_even_kernel,
        grid=(m // tm,),
        in_specs=[pl.BlockSpec((tm, ka), lambda i: (i, 0)),
                  pl.BlockSpec((tm, kb), lambda i: (i, 0)),
                  pl.BlockSpec((ka + kb, d), lambda i: (0, 0)),
                  pl.BlockSpec((tm, d), lambda i: (i, 0)),
                  mod, vec, vec, mod, mod],
        out_specs=[pl.BlockSpec((tm, d), lambda i: (i, 0)),
                   pl.BlockSpec((tm, d), lambda i: (i, 0))],
        out_shape=[jax.ShapeDtypeStruct((m, d), F32), jax.ShapeDtypeStruct((m, d), BF16)],
        compiler_params=_params("arbitrary"),
        name="out_proj_even",
    )(ya, yb, w, x, gt, g, b, sc, sh)


def _out_odd_kernel(a_ref, w_ref, x_ref, gt_ref, g_ref, b_ref, sc_ref, sh_ref, xo_ref, uo_ref, acc_ref):
    k = pl.program_id(1)

    @pl.when(k == 0)
    def _():
        acc_ref[...] = jnp.zeros_like(acc_ref)

    acc_ref[...] += _dot(a_ref[...], w_ref[...])

    @pl.when(k == pl.num_programs(1) - 1)
    def _():
        y = _post_norm(acc_ref[...], x_ref[...], gt_ref[0], g_ref[...], b_ref[...])
        xo_ref[...] = y
        uo_ref[...] = (y * (1.0 + sc_ref[0]) + sh_ref[0]).astype(uo_ref.dtype)


def _out_odd(a, w, x, gt, g, b, sc, sh, rows_per_mod, tm, tk):
    m, d = x.shape
    kdim = a.shape[1]
    r = gt.shape[1]
    per = rows_per_mod // tm
    mod = pl.BlockSpec((1, r, d), lambda i, k: (i // per, 0, 0))
    vec = pl.BlockSpec((1, d), lambda i, k: (0, 0))
    return pl.pallas_call(
        _out_odd_kernel,
        grid=(m // tm, kdim // tk),
        in_specs=[pl.BlockSpec((tm, tk), lambda i, k: (i, k)),
                  pl.BlockSpec((tk, d), lambda i, k: (k, 0)),
                  pl.BlockSpec((tm, d), lambda i, k: (i, 0)),
                  mod, vec, vec, mod, mod],
        out_specs=[pl.BlockSpec((tm, d), lambda i, k: (i, 0)),
                   pl.BlockSpec((tm, d), lambda i, k: (i, 0))],
        out_shape=[jax.ShapeDtypeStruct((m, d), F32), jax.ShapeDtypeStruct((m, d), BF16)],
        scratch_shapes=[pltpu.VMEM((tm, d), F32)],
        compiler_params=_params("arbitrary", "arbitrary"),
        name="out_proj_odd",
    )(a, w, x, gt, g, b, sc, sh)


CONV_A_HALO = 32
CONV_A_ROWS = 64


def _group_norm_gate(y, gate, g, b):
    outs = []
    for s in range(y.shape[1] // GROUP_A):
        sl = slice(s * GROUP_A, (s + 1) * GROUP_A)
        seg = y[:, sl]
        mu = jnp.mean(seg, axis=-1, keepdims=True)
        sc = seg - mu
        var = jnp.mean(sc * sc, axis=-1, keepdims=True)
        yn = sc * lax.rsqrt(var + LN_EPS) * g[:, sl] + b[:, sl]
        outs.append(_silu(yn) * _silu(gate[:, sl]))
    return outs[0] if len(outs) == 1 else jnp.concatenate(outs, axis=1)


def _conv_a_kernel(val_ref, glu_ref, gate_ref, w_ref, g_ref, b_ref, y_ref, st_ref, ext_ref):
    t = pl.program_id(2)
    tt = val_ref.shape[0]
    kw = w_ref.shape[0]
    off = CONV_A_HALO - (kw - 1)

    @pl.when(t == 0)
    def _():
        ext_ref[0:CONV_A_HALO, :] = jnp.zeros((CONV_A_HALO, ext_ref.shape[1]), F32)

    ext_ref[CONV_A_HALO:, :] = val_ref[...] * _sigmoid(glu_ref[...])
    sub = 8
    for r0 in range(0, tt, CONV_A_ROWS):
        acc = None
        for res in range(sub):
            part = None
            for j in range(kw):
                if (off + j) % sub == res:
                    a0 = r0 + off + j - res
                    term = w_ref[j:j + 1, :] * ext_ref[a0:a0 + CONV_A_ROWS + (sub if res else 0), :]
                    part = term if part is None else part + term
            if part is not None:
                part = part[res:res + CONV_A_ROWS, :]
                acc = part if acc is None else acc + part
        y = _group_norm_gate(acc, gate_ref[r0:r0 + CONV_A_ROWS, :], g_ref[...], b_ref[...])
        y_ref[r0:r0 + CONV_A_ROWS, :] = y.astype(y_ref.dtype)

    @pl.when(t == pl.num_programs(2) - 1)
    def _():
        st_ref[0] = ext_ref[tt + off:tt + CONV_A_HALO, :]

    ext_ref[0:CONV_A_HALO, :] = ext_ref[tt:tt + CONV_A_HALO, :]


def _conv_a_prompt(h, conv_w, gn_g, gn_b, batch, seq, tt=256, tc=256):
    nt = seq // tt
    nc = W_A // tc
    kw = conv_w.shape[0]
    col = lambda base: pl.BlockSpec((tt, tc), lambda b, c, t: (b * nt + t, base * nc + c))
    vec = pl.BlockSpec((1, tc), lambda b, c, t: (0, c))
    return pl.pallas_call(
        _conv_a_kernel,
        grid=(batch, nc, nt),
        in_specs=[col(0), col(1), col(2),
                  pl.BlockSpec((kw, tc), lambda b, c, t: (0, c)), vec, vec],
        out_specs=[pl.BlockSpec((tt, tc), lambda b, c, t: (b * nt + t, c)),
                   pl.BlockSpec((1, kw - 1, tc), lambda b, c, t: (b, 0, c))],
        out_shape=[jax.ShapeDtypeStruct((batch * seq, W_A), BF16),
                   jax.ShapeDtypeStruct((batch, kw - 1, W_A), F32)],
        scratch_shapes=[pltpu.VMEM((CONV_A_HALO + tt, tc), F32)],
        compiler_params=_params("arbitrary", "arbitrary", "arbitrary"),
        name="conv_a_prompt",
    )(h, h, h, conv_w, gn_g.reshape(1, -1), gn_b.reshape(1, -1))


def _conv_a_sample_kernel(h_ref, st_ref, w_ref, g_ref, b_ref, y_ref, so_ref):
    nseq, kw1, _ = st_ref.shape
    glu = h_ref[:, 0:W_A] * _sigmoid(h_ref[:, W_A:2 * W_A])
    rows = []
    for s in range(nseq):
        conv = jnp.sum(st_ref[s] * w_ref[0:kw1, :], axis=0, keepdims=True) + glu[s:s + 1, :] * w_ref[kw1:kw1 + 1, :]
        rows.append(conv)
        so_ref[s, 0:kw1 - 1, :] = st_ref[s, 1:kw1, :]
        so_ref[s, kw1 - 1:kw1, :] = glu[s:s + 1, :]
    rows.append(jnp.zeros((h_ref.shape[0] - nseq, W_A), F32))
    conv = jnp.concatenate(rows, axis=0)
    y = _group_norm_gate(conv, h_ref[:, 2 * W_A:3 * W_A], g_ref[...], b_ref[...])
    y_ref[...] = y.astype(y_ref.dtype)


def _conv_a_sample(h_s, state, conv_w, gn_g, gn_b):
    rows = h_s.shape[0]
    return pl.pallas_call(
        _conv_a_sample_kernel,
        out_shape=[jax.ShapeDtypeStruct((rows, W_A), BF16), jax.ShapeDtypeStruct(state.shape, F32)],
        compiler_params=pltpu.CompilerParams(vmem_limit_bytes=VMEM_LIMIT),
        name="conv_a_sample",
    )(h_s, state, conv_w, gn_g.reshape(1, -1), gn_b.reshape(1, -1))


def _attn_prompt_kernel(bias_ref, q_ref, k_ref, v_ref, gate_ref, o_ref, acc_ref, later_ref):
    h = pl.program_id(1)
    i = pl.program_id(2)
    blk = q_ref.shape[0]
    bias = bias_ref[h]
    q = (q_ref[...] * (SB_HEAD_DIM ** -0.5)).astype(BF16)
    row = lax.broadcasted_iota(jnp.int32, (blk, blk), 0)
    col = lax.broadcasted_iota(jnp.int32, (blk, blk), 1)
    after = jnp.where(row > col, 1.0, 0.0).astype(BF16)
    visible = col < row

    acc_ref[...] = jnp.zeros_like(acc_ref)
    later_ref[...] = jnp.zeros_like(later_ref)

    def blocks(js, masked):
        zs, sps = [], []
        for j in js:
            start = pl.multiple_of(j * blk, blk)
            z = _dot(q, k_ref[pl.ds(start, blk), :].astype(BF16), NT) + bias
            sp = _softplus(z)
            if masked:
                sp = jnp.where(visible, sp, 0.0)
            zs.append(z)
            sps.append(sp)
        later = later_ref[...]
        out = None
        for j, z, sp in zip(js, zs, sps):
            start = pl.multiple_of(j * blk, blk)
            inside = _dot(sp.astype(BF16), after)
            w = jnp.exp(z - sp - inside - later)
            if masked:
                w = jnp.where(visible, w, 0.0)
            pv = _dot(w.astype(BF16), v_ref[pl.ds(start, blk), :].astype(BF16))
            out = pv if out is None else out + pv
            later = later + jnp.sum(sp, axis=-1, keepdims=True)
        acc_ref[...] += out
        later_ref[...] = later

    blocks([i], True)

    def body(jj, carry):
        j = i - 1 - 4 * jj
        blocks([j, j - 1, j - 2, j - 3], False)
        return carry

    lax.fori_loop(0, i // 4, body, 0)
    rem = i % 4

    @pl.when(rem >= 2)
    def _():
        blocks([rem - 1, rem - 2], False)

    @pl.when(rem % 2 == 1)
    def _():
        blocks([0], False)

    o_ref[...] = (acc_ref[...] * _silu(gate_ref[...])).astype(o_ref.dtype)


def _attn_prompt(h, sb_bias, batch, seq, blk=256):
    nq = seq // blk
    base = 3 * W_A // SB_HEAD_DIM
    return pl.pallas_call(
        _attn_prompt_kernel,
        grid_spec=pltpu.PrefetchScalarGridSpec(
            num_scalar_prefetch=0,
            grid=(batch, SB_HEADS, nq),
            in_specs=[pl.BlockSpec(memory_space=pltpu.SMEM),
                      pl.BlockSpec((blk, SB_HEAD_DIM), lambda b, hh, i: (b * nq + i, base + hh)),
                      pl.BlockSpec((seq, SB_HEAD_DIM), lambda b, hh, i: (b, base + SB_HEADS + hh)),
                      pl.BlockSpec((seq, SB_HEAD_DIM), lambda b, hh, i: (b, base + 2 * SB_HEADS + hh)),
                      pl.BlockSpec((blk, SB_HEAD_DIM), lambda b, hh, i: (b * nq + i, base + 3 * SB_HEADS + hh))],
            out_specs=pl.BlockSpec((blk, SB_HEAD_DIM), lambda b, hh, i: (b * nq + i, hh)),
            scratch_shapes=[pltpu.VMEM((blk, SB_HEAD_DIM), F32), pltpu.VMEM((blk, 1), F32)]),
        out_shape=jax.ShapeDtypeStruct((batch * seq, W_B), BF16),
        compiler_params=_params("arbitrary", "arbitrary", "arbitrary"),
        name="attn_prompt",
    )(sb_bias, h, h, h, h)


PAGES_PER_STEP = 8


def _attn_sample_kernel(pt_ref, q_ref, bias_ref, gate_ref, own_ref, tok_ref, tokt_ref, *refs):
    npg = PAGES_PER_STEP
    k_refs, v_refs = refs[:npg], refs[npg:2 * npg]
    o_ref, acc_ref, later_ref = refs[2 * npg:]
    p = pl.program_id(1)
    rows = q_ref.shape[1]
    page = k_refs[0].shape[2] // SB_HEADS

    @pl.when(p == 0)
    def _():
        acc_ref[...] = jnp.zeros_like(acc_ref)
        later_ref[...] = jnp.zeros_like(later_ref)

    q = (q_ref[0] * (SB_HEAD_DIM ** -0.5)).astype(BF16)
    own = own_ref[...]
    r2 = lax.broadcasted_iota(jnp.int32, (page, page), 0)
    c2 = lax.broadcasted_iota(jnp.int32, (page, page), 1)
    after = jnp.where(r2 > c2, 1.0, 0.0).astype(BF16)

    def two_pass(x, m):
        hi, lo = _split(x)
        both = _dot(jnp.concatenate([hi, lo], axis=0), m)
        return both[0:rows, :] + both[rows:, :]

    pages = range(npg)
    zs = [two_pass(_dot(q, k_refs[r][0, 0].astype(BF16), NT) * own, tok_ref[...]) + bias_ref[...] for r in pages]
    sps = [_softplus(z) for z in zs]
    inside = [_dot(sp.astype(BF16), after) for sp in sps]
    later = later_ref[...]
    ws = []
    for r in pages:
        ws.append(jnp.exp(zs[r] - sps[r] - inside[r] - later).astype(BF16))
        later = later + jnp.sum(sps[r], axis=-1, keepdims=True)
    spread = [(_dot(w, tokt_ref[...]) * own).astype(BF16) for w in ws]
    out = acc_ref[...]
    for r in pages:
        out = out + _dot(spread[r], v_refs[r][0, 0].astype(BF16))
    acc_ref[...] = out
    later_ref[...] = later

    @pl.when(p == pl.num_programs(1) - 1)
    def _():
        o_ref[0] = acc_ref[...] * _silu(gate_ref[0])


def _attn_sample(q, gate, bias, cache_k, cache_v, page_table, layer):
    nseq, rows, _ = q.shape
    n_pages = page_table.shape[1]
    npg = PAGES_PER_STEP
    blk = cache_k.shape[2]

    def page_spec(r):
        return pl.BlockSpec((1, 1, blk, SB_HEAD_DIM),
                            lambda s, p, pt: (layer, pt[s, n_pages - 1 - (p * npg + r)], 0, 0))

    row_spec = pl.BlockSpec((1, rows, SB_HEAD_DIM), lambda s, p, pt: (s, 0, 0))
    page = blk // SB_HEADS
    r_head = jnp.arange(blk, dtype=jnp.int32) % SB_HEADS
    r_tok = jnp.arange(blk, dtype=jnp.int32) // SB_HEADS
    own = (jnp.arange(rows, dtype=jnp.int32)[:, None] == r_head[None, :]).astype(F32)
    tok = (r_tok[:, None] == jnp.arange(page, dtype=jnp.int32)[None, :]).astype(BF16)
    whole = lambda shape: pl.BlockSpec(shape, lambda s, p, pt: (0, 0))
    return pl.pallas_call(
        _attn_sample_kernel,
        grid_spec=pltpu.PrefetchScalarGridSpec(
            num_scalar_prefetch=1,
            grid=(nseq, n_pages // npg),
            in_specs=[row_spec, whole((rows, 1)), row_spec, whole((rows, blk)), whole((blk, page)), whole((page, blk))]
                     + [page_spec(r) for r in range(npg)] + [page_spec(r) for r in range(npg)],
            out_specs=row_spec,
            scratch_shapes=[pltpu.VMEM((rows, SB_HEAD_DIM), F32), pltpu.VMEM((rows, 1), F32)]),
        out_shape=jax.ShapeDtypeStruct((nseq, rows, SB_HEAD_DIM), F32),
        compiler_params=_params("arbitrary", "arbitrary"),
        name="attn_sample",
    )(page_table, q, bias, gate, own, tok, tok.T, *([cache_k] * npg), *([cache_v] * npg))


def _gdn_gate_kernel(u_ref, wt_ref, alog_ref, dt_ref, beta_ref, gc_ref):
    raw = _dot(wt_ref[...], u_ref[...], NT)
    hv = raw.shape[0] // 2
    beta_ref[0] = _sigmoid(raw[0:hv, :])
    g = -jnp.exp(alog_ref[...]) * _softplus(raw[hv:, :] + dt_ref[...])
    r2 = lax.broadcasted_iota(jnp.int32, (GDN_CHUNK, GDN_CHUNK), 0)
    c2 = lax.broadcasted_iota(jnp.int32, (GDN_CHUNK, GDN_CHUNK), 1)
    upto = jnp.where(r2 <= c2, 1.0, 0.0).astype(BF16)
    gh, gl = _split(g)
    gl2 = (g - gh.astype(F32) - gl.astype(F32)).astype(BF16)
    for c in range(g.shape[1] // GDN_CHUNK):
        sl = slice(c * GDN_CHUNK, (c + 1) * GDN_CHUNK)
        gc_ref[0, :, sl] = _dot(gh[:, sl], upto) + (_dot(gl[:, sl], upto) + _dot(gl2[:, sl], upto))


def _gdn_gates(u, w_ba_t, a_log, dt_bias, batch, seq, tt=512):
    nt = seq // tt
    hv2, d = w_ba_t.shape
    hv = hv2 // 2
    out = pl.BlockSpec((1, hv, tt), lambda b, t: (b, 0, t))
    colv = pl.BlockSpec((hv, 1), lambda b, t: (0, 0))
    return pl.pallas_call(
        _gdn_gate_kernel,
        grid=(batch, nt),
        in_specs=[pl.BlockSpec((tt, d), lambda b, t: (b * nt + t, 0)),
                  pl.BlockSpec((hv2, d), lambda b, t: (0, 0)), colv, colv],
        out_specs=[out, out],
        out_shape=[jax.ShapeDtypeStruct((batch, hv, seq), F32)] * 2,
        compiler_params=_params("arbitrary", "arbitrary"),
        name="gdn_gates",
    )(u, w_ba_t, a_log.reshape(hv, 1), dt_bias.reshape(hv, 1))


CONV_C_HALO = 8


def _qk_norm(y, cblk, tc):
    first = cblk * tc
    is_q = first < GDN_QK_W
    is_qk = first < 2 * GDN_QK_W
    outs = []
    for s in range(tc // GDN_HEAD_DIM):
        seg = y[:, s * GDN_HEAD_DIM:(s + 1) * GDN_HEAD_DIM]
        rs = lax.rsqrt(jnp.sum(seg * seg, axis=-1, keepdims=True) + RMS_EPS)
        f = jnp.where(is_q, rs * (GDN_HEAD_DIM ** -0.5), jnp.where(is_qk, rs, 1.0))
        outs.append(seg * f)
    return outs[0] if len(outs) == 1 else jnp.concatenate(outs, axis=1)


def _conv_c_kernel(x_ref, w_ref, y_ref, st_ref, ext_ref):
    c = pl.program_id(1)
    t = pl.program_id(2)
    tt, tc = x_ref.shape
    kw = w_ref.shape[0]
    off = CONV_C_HALO - (kw - 1)

    @pl.when(t == 0)
    def _():
        ext_ref[0:CONV_C_HALO, :] = jnp.zeros((CONV_C_HALO, tc), F32)

    ext_ref[CONV_C_HALO:, :] = x_ref[...]
    acc = w_ref[0:1, :] * ext_ref[off:off + tt, :]
    for j in range(1, kw):
        acc = acc + w_ref[j:j + 1, :] * ext_ref[off + j:off + j + tt, :]
    y_ref[...] = _qk_norm(_silu(acc), c, tc)

    @pl.when(t == pl.num_programs(2) - 1)
    def _():
        st_ref[0] = ext_ref[tt + off:tt + CONV_C_HALO, :]

    ext_ref[0:CONV_C_HALO, :] = ext_ref[tt:tt + CONV_C_HALO, :]


def _conv_c_prompt(h, conv_w, batch, seq, tt=256, tc=512):
    nt = seq // tt
    nc = GDN_CONV_DIM // tc
    kw = conv_w.shape[0]
    return pl.pallas_call(
        _conv_c_kernel,
        grid=(batch, nc, nt),
        in_specs=[pl.BlockSpec((tt, tc), lambda b, c, t: (b * nt + t, c)),
                  pl.BlockSpec((kw, tc), lambda b, c, t: (0, c))],
        out_specs=[pl.BlockSpec((tt, tc), lambda b, c, t: (b * nt + t, c)),
                   pl.BlockSpec((1, kw - 1, tc), lambda b, c, t: (b, 0, c))],
        out_shape=[jax.ShapeDtypeStruct((batch * seq, GDN_CONV_DIM), F32),
                   jax.ShapeDtypeStruct((batch, kw - 1, GDN_CONV_DIM), F32)],
        scratch_shapes=[pltpu.VMEM((CONV_C_HALO + tt, tc), F32)],
        compiler_params=_params("arbitrary", "arbitrary", "arbitrary"),
        name="conv_c_prompt",
    )(h, conv_w)


def _gdn_prompt_kernel(q_ref, k_ref, v_ref, z_ref, beta_ref, gc_ref, gw_ref, o_ref, s_out_ref, s_ref):
    t = pl.program_id(2)
    c = GDN_CHUNK
    tt = q_ref.shape[0]
    rep = v_ref.shape[1] // GDN_HEAD_DIM

    @pl.when(t == 0)
    def _():
        s_ref[...] = jnp.zeros_like(s_ref)

    row = lax.broadcasted_iota(jnp.int32, (c, c), 0)
    col = lax.broadcasted_iota(jnp.int32, (c, c), 1)
    incl = row >= col
    strict = row > col
    eye = jnp.where(row == col, 1.0, 0.0)
    merge = []
    m = 1
    while m < c:
        bi, bj = row // m, col // m
        merge.append(jnp.logical_and(bi - bj == 1, bj % 2 == 0))
        m *= 2

    bf = lambda x: x.astype(BF16)
    nchunk = tt // c
    units = [(n, hv) for n in range(nchunk) for hv in range(rep)]
    rows_of = lambda n: slice(n * c, (n + 1) * c)
    cols_of = lambda hv: slice(hv * GDN_HEAD_DIM, (hv + 1) * GDN_HEAD_DIM)

    kq = {}
    for n in range(nchunk):
        kb = bf(k_ref[rows_of(n), :])
        kq[n] = _dot(jnp.concatenate([kb, bf(q_ref[rows_of(n), :])], axis=0), kb, NT)
    g_row, b_row, g_col, b_col, decay, lmat, xo = {}, {}, {}, {}, {}, {}, {}
    for un in units:
        n, hv = un
        g_row[un] = jnp.broadcast_to(gc_ref[0, hv, :, rows_of(n)], (c, c))
        b_row[un] = jnp.broadcast_to(beta_ref[0, hv, :, rows_of(n)], (c, c))
        g_col[un] = g_row[un].T
        b_col[un] = b_row[un].T
        decay[un] = jnp.where(incl, jnp.exp(jnp.where(incl, g_col[un] - g_row[un], 0.0)), 0.0)
        lmat[un] = jnp.where(strict, kq[n][0:c, :] * b_col[un] * decay[un], 0.0)
        xo[un] = -jnp.where(merge[0], lmat[un], 0.0)
    for lvl in merge[1:]:
        for un in units:
            cm = jnp.where(lvl, lmat[un], 0.0)
            y = cm + _dot(bf(xo[un]), bf(cm))
            xo[un] = xo[un] - (y + _dot(bf(y), bf(xo[un])))
    u_val, wq, intra, k_dec_t, g_last = {}, {}, {}, {}, {}
    for un in units:
        n, hv = un
        k = k_ref[rows_of(n), :]
        v = v_ref[rows_of(n), cols_of(hv)]
        xb = xo[un] * b_row[un]
        u_val[un] = v * b_col[un] + _dot(bf(xb), bf(v))
        eg_col = jnp.exp(g_col[un])
        w_key = k * (b_col[un] * eg_col) + _dot(bf(xb * jnp.exp(g_row[un])), bf(k))
        wq[un] = bf(jnp.concatenate([w_key, q_ref[rows_of(n), :] * eg_col], axis=0))
        intra[un] = bf(jnp.where(incl, kq[n][c:, :] * decay[un], 0.0))
        g_last[un] = g_col[un][c - 1:c, :]
        k_dec_t[un] = bf((k * jnp.exp(g_last[un] - g_col[un])).T)

    for un in units:
        n, hv = un
        s = s_ref[hv]
        ws = _dot(wq[un], bf(s))
        v_new = bf(u_val[un] - ws[0:c, :])
        o = ws[c:, :] + _dot(intra[un], v_new)
        s_ref[hv] = s * jnp.exp(g_last[un]) + _dot(k_dec_t[un], v_new)
        o = o * lax.rsqrt(jnp.mean(o * o, axis=-1, keepdims=True) + RMS_EPS) * gw_ref[...]
        o_ref[rows_of(n), cols_of(hv)] = (o * _silu(z_ref[rows_of(n), cols_of(hv)])).astype(o_ref.dtype)

    @pl.when(t == pl.num_programs(2) - 1)
    def _():
        s_out_ref[0] = s_ref[...]


def _gdn_prompt(qkv, h, beta, gc, gnorm_w, batch, seq, tt=512):
    nt = seq // tt
    rep = GDN_V_HEADS // GDN_QK_HEADS
    dh = GDN_HEAD_DIM
    vw = rep * dh
    gate = pl.BlockSpec((1, rep, 1, tt), lambda b, hq, t: (b, hq, 0, t))
    return pl.pallas_call(
        _gdn_prompt_kernel,
        grid=(batch, GDN_QK_HEADS, nt),
        in_specs=[pl.BlockSpec((tt, dh), lambda b, hq, t: (b * nt + t, hq)),
                  pl.BlockSpec((tt, dh), lambda b, hq, t: (b * nt + t, GDN_QK_HEADS + hq)),
                  pl.BlockSpec((tt, vw), lambda b, hq, t: (b * nt + t, 2 * GDN_QK_W // vw + hq)),
                  pl.BlockSpec((tt, vw), lambda b, hq, t: (b * nt + t, GDN_CONV_DIM // vw + hq)),
                  gate, gate,
                  pl.BlockSpec((1, dh), lambda b, hq, t: (0, 0))],
        out_specs=[pl.BlockSpec((tt, vw), lambda b, hq, t: (b * nt + t, hq)),
                   pl.BlockSpec((1, rep, dh, dh), lambda b, hq, t: (b, hq, 0, 0))],
        out_shape=[jax.ShapeDtypeStruct((batch * seq, GDN_V_W), BF16),
                   jax.ShapeDtypeStruct((batch, GDN_V_HEADS, dh, dh), F32)],
        scratch_shapes=[pltpu.VMEM((rep, dh, dh), F32)],
        compiler_params=_params("arbitrary", "arbitrary", "arbitrary"),
        name="gdn_prompt",
    )(qkv, qkv, qkv, h, beta, gc, gnorm_w.reshape(1, dh))


def _conv_c_sample_kernel(h_ref, st_ref, w_ref, u_ref, wba_ref, alog_ref, dt_ref, y_ref, so_ref, beta_ref, eg_ref):
    nseq, kw1, width = st_ref.shape
    tc = 1024
    for c0 in range(0, width, tc):
        cs = slice(c0, c0 + tc)
        rows = []
        for s in range(nseq):
            x_new = h_ref[s:s + 1, cs]
            conv = jnp.sum(st_ref[s, :, cs] * w_ref[0:kw1, cs], axis=0, keepdims=True) + x_new * w_ref[kw1:kw1 + 1, cs]
            rows.append(conv)
            so_ref[s, 0:kw1 - 1, cs] = st_ref[s, 1:kw1, cs]
            so_ref[s, kw1 - 1:kw1, cs] = x_new
        rows.append(jnp.zeros((h_ref.shape[0] - nseq, tc), F32))
        y_ref[:, cs] = _qk_norm(_silu(jnp.concatenate(rows, axis=0)), c0 // tc, tc)
    raw = _dot(u_ref[...], wba_ref[...])
    hv = raw.shape[1] // 2
    beta_ref[...] = _sigmoid(raw[:, 0:hv])
    eg_ref[...] = jnp.exp(-jnp.exp(alog_ref[...]) * _softplus(raw[:, hv:] + dt_ref[...]))


def _conv_c_sample(h_s, state, conv_w, u_s, w_ba, a_log, dt_bias):
    rows = h_s.shape[0]
    hv = a_log.shape[0]
    return pl.pallas_call(
        _conv_c_sample_kernel,
        out_shape=[jax.ShapeDtypeStruct((rows, GDN_CONV_DIM), F32), jax.ShapeDtypeStruct(state.shape, F32),
                   jax.ShapeDtypeStruct((rows, hv), F32), jax.ShapeDtypeStruct((rows, hv), F32)],
        compiler_params=pltpu.CompilerParams(vmem_limit_bytes=VMEM_LIMIT),
        name="conv_c_sample",
    )(h_s, state, conv_w, u_s, w_ba, a_log.reshape(1, hv), dt_bias.reshape(1, hv))


def _gdn_sample_kernel(beta_ref, eg_ref, qc_ref, kc_ref, v_ref, z_ref, gw_ref, s_ref, o_ref, so_ref):
    sq = pl.program_id(0)
    rep = GDN_V_HEADS // GDN_QK_HEADS
    for hv in range(GDN_V_HEADS):
        hq = hv // rep
        k_col = kc_ref[0, :, hq:hq + 1]
        q_col = qc_ref[0, :, hq:hq + 1]
        s = s_ref[0, hv] * eg_ref[sq, hv]
        delta = (v_ref[0, hv:hv + 1, :] - jnp.sum(s * k_col, axis=0, keepdims=True)) * beta_ref[sq, hv]
        s = s + k_col * delta
        so_ref[0, hv] = s
        o = jnp.sum(s * q_col, axis=0, keepdims=True)
        o = o * lax.rsqrt(jnp.mean(o * o, axis=-1, keepdims=True) + RMS_EPS) * gw_ref[...]
        o_ref[0, hv:hv + 1, :] = o * _silu(z_ref[0, hv:hv + 1, :])


def _gdn_sample(beta, eg, q_col, k_col, v, z, gnorm_w, state):
    nseq, hv, dk, dv = state.shape
    hq = q_col.shape[2]
    smem = pl.BlockSpec(memory_space=pltpu.SMEM)
    colspec = pl.BlockSpec((1, dk, hq), lambda s: (s, 0, 0))
    rowspec = pl.BlockSpec((1, hv, dv), lambda s: (s, 0, 0))
    stspec = pl.BlockSpec((1, hv, dk, dv), lambda s: (s, 0, 0, 0))
    return pl.pallas_call(
        _gdn_sample_kernel,
        grid=(nseq,),
        in_specs=[smem, smem, colspec, colspec, rowspec, rowspec,
                  pl.BlockSpec((1, dv), lambda s: (0, 0)), stspec],
        out_specs=[rowspec, stspec],
        out_shape=[jax.ShapeDtypeStruct((nseq, hv, dv), F32), jax.ShapeDtypeStruct(state.shape, F32)],
        compiler_params=_params("arbitrary"),
        name="gdn_sample",
    )(beta, eg, q_col, k_col, v, z, gnorm_w.reshape(1, dv), state)


def kernel(x_prompt, x_sample, c_prompt, c_sample, cache_k, cache_v, page_table, state_conv_a, state_conv_c, state_delta, w_ada, b_ada, ln_g, ln_b, w_in_even, conv_w_a, gn_g_a, gn_b_a, sb_bias, w_out_even, w_in_odd, conv_w_c, a_log_c, dt_bias_c, gnorm_w_c, w_out_odd):
    batch, seq, d = x_prompt.shape
    nseq = x_sample.shape[0]
    n_even, n_pool, page_size = cache_k.shape[:3]
    np_rows = batch * seq
    pad = SAMPLE_ROWS - nseq

    c_all = jnp.concatenate([c_prompt, c_sample, jnp.zeros((SAMPLE_ROWS - batch - nseq, d), F32)], axis=0)
    mod = _ada_all(c_all, w_ada, b_ada)

    def mods(layer):
        m = mod[layer]
        parts = [m[:, i * d:(i + 1) * d] for i in range(3)]
        pm = [p[0:batch].reshape(batch, 1, d) for p in parts]
        sm = [jnp.pad(p[batch:batch + nseq], ((0, pad), (0, 0))).reshape(1, SAMPLE_ROWS, d) for p in parts]
        return pm, sm

    xp = x_prompt.reshape(np_rows, d)
    xs = jnp.pad(x_sample.reshape(nseq, d), ((0, pad), (0, 0)))
    (sh_p, sc_p, _), (sh_s, sc_s, _) = mods(0)
    up = _modulate(xp, sc_p, sh_p, seq, 512)
    us = _modulate(xs, sc_s, sh_s, SAMPLE_ROWS, SAMPLE_ROWS)

    ck = cache_k.reshape(n_even, n_pool, page_size * SB_HEADS, SB_HEAD_DIM)
    cv = cache_v.reshape(n_even, n_pool, page_size * SB_HEADS, SB_HEAD_DIM)
    bias_pad = lambda b: jnp.pad(b, (0, SAMPLE_ROWS - SB_HEADS)).reshape(SAMPLE_ROWS, 1)
    heads_pad = lambda a: jnp.pad(a.reshape(nseq, SB_HEADS, SB_HEAD_DIM), ((0, 0), (0, SAMPLE_ROWS - SB_HEADS), (0, 0)))

    kp_l, vp_l, ks_l, vs_l = [], [], [], []
    cap_l, cas_l, ccp_l, ccs_l, dp_l, ds_l = [], [], [], [], [], []
    for layer in range(DEPTH):
        j = layer // 2
        (_, _, gt_p), (_, _, gt_s) = mods(layer)
        if layer + 1 < DEPTH:
            (shn_p, scn_p, _), (shn_s, scn_s, _) = mods(layer + 1)
        else:
            shn_p = scn_p = jnp.zeros((batch, 1, d), F32)
            shn_s = scn_s = jnp.zeros((1, SAMPLE_ROWS, d), F32)
        g_ln = ln_g[layer].reshape(1, d)
        b_ln = ln_b[layer].reshape(1, d)
        if layer % 2 == 0:
            w_out = w_out_even[j].astype(BF16)
            hp, hs = _in_proj(up, us, w_in_even, j, w_in_even.shape[2], min(1024, np_rows), 1024)
            ya_p, buf_p = _conv_a_prompt(hp, conv_w_a[j], gn_g_a[j], gn_b_a[j], batch, seq)
            yb_p = _attn_prompt(hp, sb_bias[j], batch, seq)
            xp, up = _out_even(ya_p, yb_p, w_out, xp, gt_p, g_ln, b_ln, scn_p, shn_p, seq, 256)
            kp_l.append(hp[:, 3 * W_A + W_B:3 * W_A + 2 * W_B].reshape(batch, seq, SB_HEADS, SB_HEAD_DIM))
            vp_l.append(hp[:, 3 * W_A + 2 * W_B:3 * W_A + 3 * W_B].reshape(batch, seq, SB_HEADS, SB_HEAD_DIM))
            cap_l.append(buf_p)
            ya_s, buf_s = _conv_a_sample(hs, state_conv_a[j], conv_w_a[j], gn_g_a[j], gn_b_a[j])
            q_s = heads_pad(hs[:nseq, 3 * W_A:3 * W_A + W_B])
            gate_s = heads_pad(hs[:nseq, 3 * W_A + 3 * W_B:3 * W_A + 4 * W_B])
            o_s = _attn_sample(q_s, gate_s, bias_pad(sb_bias[j]), ck, cv, page_table, j)
            yb_s = jnp.pad(o_s[:, :SB_HEADS].reshape(nseq, W_B), ((0, pad), (0, 0))).astype(BF16)
            xs, us = _out_even(ya_s, yb_s, w_out, xs, gt_s, g_ln, b_ln, scn_s, shn_s, SAMPLE_ROWS, SAMPLE_ROWS)
            ks_l.append(hs[:nseq, 3 * W_A + W_B:3 * W_A + 2 * W_B].reshape(nseq, 1, SB_HEADS, SB_HEAD_DIM))
            vs_l.append(hs[:nseq, 3 * W_A + 2 * W_B:3 * W_A + 3 * W_B].reshape(nseq, 1, SB_HEADS, SB_HEAD_DIM))
            cas_l.append(buf_s)
        else:
            w_out = w_out_odd[j].astype(BF16)
            n_main = GDN_CONV_DIM + GDN_V_W
            w_ba = w_in_odd[j, :, n_main:].astype(BF16)
            hp, hs = _in_proj(up, us, w_in_odd, j, n_main, min(1024, np_rows), 1024)
            beta_p, gc_p = _gdn_gates(up, w_ba.T, a_log_c[j], dt_bias_c[j], batch, seq)
            qkv_p, buf_p = _conv_c_prompt(hp, conv_w_c[j], batch, seq)
            o_p, s_p = _gdn_prompt(qkv_p, hp, beta_p.reshape(batch, GDN_V_HEADS, 1, seq),
                                   gc_p.reshape(batch, GDN_V_HEADS, 1, seq), gnorm_w_c[j], batch, seq)
            xp, up = _out_odd(o_p, w_out, xp, gt_p, g_ln, b_ln, scn_p, shn_p, seq, 512, 1024)
            ccp_l.append(buf_p)
            dp_l.append(s_p)
            qkv_s, buf_s, beta_s, eg_s = _conv_c_sample(hs, state_conv_c[j], conv_w_c[j], us, w_ba,
                                                        a_log_c[j], dt_bias_c[j])
            heads_col = lambda a: jnp.swapaxes(a.reshape(nseq, GDN_QK_HEADS, GDN_HEAD_DIM), 1, 2)
            q_col = heads_col(qkv_s[:nseq, 0:GDN_QK_W])
            k_col = heads_col(qkv_s[:nseq, GDN_QK_W:2 * GDN_QK_W])
            v_s = qkv_s[:nseq, 2 * GDN_QK_W:].reshape(nseq, GDN_V_HEADS, GDN_HEAD_DIM)
            z_s = hs[:nseq, GDN_CONV_DIM:n_main].reshape(nseq, GDN_V_HEADS, GDN_HEAD_DIM)
            o_s, s_s = _gdn_sample(beta_s[:nseq], eg_s[:nseq], q_col, k_col, v_s, z_s, gnorm_w_c[j], state_delta[j])
            a_s = jnp.pad(o_s.reshape(nseq, GDN_V_W), ((0, pad), (0, 0))).astype(BF16)
            xs, us = _out_odd(a_s, w_out, xs, gt_s, g_ln, b_ln, scn_s, shn_s, SAMPLE_ROWS, SAMPLE_ROWS, 1024)
            ccs_l.append(buf_s)
            ds_l.append(s_s)
    return (xp.reshape(batch, seq, d), xs[:nseq].reshape(nseq, 1, d),
            jnp.stack(kp_l), jnp.stack(vp_l), jnp.stack(ks_l), jnp.stack(vs_l),
            jnp.stack(cap_l), jnp.stack(cas_l), jnp.stack(ccp_l), jnp.stack(ccs_l),
            jnp.stack(dp_l), jnp.stack(ds_l))
```

```python
import functools

import jax
import jax.numpy as jnp
from jax import lax
from jax.experimental import pallas as pl
from jax.experimental.pallas import tpu as pltpu

F32 = jnp.float32
BF16 = jnp.bfloat16

D_MODEL = 2048
DEPTH = 4
CONV_WIDTH_A = 31
W_A = 1024
GROUP_A = 128
SB_HEAD_DIM = 128
SB_HEADS = 8
W_B = 1024
GDN_HEAD_DIM = 128
GDN_QK_HEADS = 16
GDN_V_HEADS = 32
GDN_QK_W = 2048
GDN_V_W = 4096
GDN_CONV_WIDTH = 4
GDN_CONV_DIM = 8192
GDN_CHUNK = 128
DEEPNORM_ALPHA = (2 * DEPTH) ** 0.25
LN_EPS = 1e-5
RMS_EPS = 1e-6
SAMPLE_ROWS = 16
LANES = 128
VMEM_LIMIT = 48 * 1024 * 1024

NN = (((1,), (0,)), ((), ()))
NT = (((1,), (1,)), ((), ()))


def _params(*sem):
    return pltpu.CompilerParams(dimension_semantics=sem, vmem_limit_bytes=VMEM_LIMIT)


def _dot(a, b, dims=NN):
    return lax.dot_general(a, b, dims, preferred_element_type=F32)


def _split(x):
    hi = x.astype(BF16)
    lo = (x - hi.astype(F32)).astype(BF16)
    return hi, lo


def _dot3(a, b, dims=NN):
    ah, al = _split(a)
    bh, bl = _split(b)
    return _dot(ah, bh, dims) + (_dot(ah, bl, dims) + _dot(al, bh, dims))


def _dot_exact_rhs(a, m_bf16):
    ah, al = _split(a)
    return _dot(ah, m_bf16) + _dot(al, m_bf16)


def _sigmoid(x):
    return 1.0 / (1.0 + jnp.exp(-x))


def _silu(x):
    return x * _sigmoid(x)


def _softplus(x):
    return jnp.maximum(x, 0.0) + jnp.log(1.0 + jnp.exp(-jnp.abs(x)))


def _ada_kernel(c_ref, w_ref, b_ref, o_ref):
    s = _silu(c_ref[...]).astype(BF16)
    o_ref[0] = _dot(s, w_ref[0].astype(BF16)) + b_ref[0]


def _ada_all(c_all, w_ada, b_ada):
    depth, d, n = w_ada.shape
    rows = c_all.shape[0]
    tn = 512
    return pl.pallas_call(
        _ada_kernel,
        grid=(depth, n // tn),
        in_specs=[pl.BlockSpec((rows, d), lambda l, j: (0, 0)),
                  pl.BlockSpec((1, d, tn), lambda l, j: (l, 0, j)),
                  pl.BlockSpec((1, 1, tn), lambda l, j: (l, 0, j))],
        out_specs=pl.BlockSpec((1, rows, tn), lambda l, j: (l, 0, j)),
        out_shape=jax.ShapeDtypeStruct((depth, rows, n), F32),
        compiler_params=_params("arbitrary", "arbitrary"),
        name="ada_mod",
    )(c_all, w_ada, b_ada.reshape(depth, 1, n))


def _modulate_kernel(x_ref, sc_ref, sh_ref, u_ref):
    u_ref[...] = (x_ref[...] * (1.0 + sc_ref[0]) + sh_ref[0]).astype(u_ref.dtype)


def _modulate(x, sc, sh, rows_per_mod, tm):
    m, d = x.shape
    r = sc.shape[1]
    per = rows_per_mod // tm
    return pl.pallas_call(
        _modulate_kernel,
        grid=(m // tm,),
        in_specs=[pl.BlockSpec((tm, d), lambda i: (i, 0)),
                  pl.BlockSpec((1, r, d), lambda i: (i // per, 0, 0)),
                  pl.BlockSpec((1, r, d), lambda i: (i // per, 0, 0))],
        out_specs=pl.BlockSpec((tm, d), lambda i: (i, 0)),
        out_shape=jax.ShapeDtypeStruct((m, d), BF16),
        compiler_params=_params("arbitrary"),
        name="modulate",
    )(x, sc, sh)


def _in_proj_kernel(a_ref, as_ref, w_ref, o_ref, os_ref, wb_ref, *, dims):
    @pl.when(pl.program_id(1) == 0)
    def _():
        wb_ref[...] = w_ref[0].astype(BF16)
        os_ref[...] = _dot(as_ref[...], wb_ref[...], dims)

    o_ref[...] = _dot(a_ref[...], wb_ref[...], dims)


def _weight_spec(k, tn, layer, col0, transposed):
    if transposed:
        return pl.BlockSpec((1, tn, k), lambda j, i: (layer, col0 + j, 0)), (tn, k), NT
    return pl.BlockSpec((1, k, tn), lambda j, i: (layer, 0, col0 + j)), (k, tn), NN


def _in_proj(a, a_s, w, layer, col0, n_out, tm, tn, transposed=False):
    m, k = a.shape
    r = a_s.shape[0]
    assert m % tm == 0 and n_out % tn == 0
    w_spec, wb_shape, dims = _weight_spec(k, tn, layer, col0, transposed)
    return pl.pallas_call(
        functools.partial(_in_proj_kernel, dims=dims),
        grid=(n_out // tn, m // tm),
        in_specs=[pl.BlockSpec((tm, k), lambda j, i: (i, 0)),
                  pl.BlockSpec((r, k), lambda j, i: (0, 0)),
                  w_spec],
        out_specs=[pl.BlockSpec((tm, tn), lambda j, i: (i, j)),
                   pl.BlockSpec((r, tn), lambda j, i: (0, j))],
        out_shape=[jax.ShapeDtypeStruct((m, n_out), F32), jax.ShapeDtypeStruct((r, n_out), F32)],
        scratch_shapes=[pltpu.VMEM(wb_shape, BF16)],
        compiler_params=_params("arbitrary", "arbitrary"),
        name="in_proj",
    )(a, a_s, w)


def _post_norm(h, x, gt, g, b):
    r = DEEPNORM_ALPHA * x + (1.0 + gt) * h
    mu = jnp.mean(r, axis=-1, keepdims=True)
    rc = r - mu
    var = jnp.mean(rc * rc, axis=-1, keepdims=True)
    return rc * lax.rsqrt(var + LN_EPS) * g + b


def _out_even_kernel(ya_ref, yb_ref, w_ref, x_ref, gt_ref, g_ref, b_ref, sc_ref, sh_ref, xo_ref, uo_ref):
    ka = ya_ref.shape[1]
    h = _dot(ya_ref[...], w_ref[0:ka, :]) + _dot(yb_ref[...], w_ref[ka:, :])
    y = _post_norm(h, x_ref[...], gt_ref[0], g_ref[...], b_ref[...])
    xo_ref[...] = y
    uo_ref[...] = (y * (1.0 + sc_ref[0]) + sh_ref[0]).astype(uo_ref.dtype)


def _out_even(ya, yb, w, x, gt, g, b, sc, sh, rows_per_mod, tm):
    m, d = x.shape
    ka, kb = ya.shape[1], yb.shape[1]
    r = gt.shape[1]
    per = rows_per_mod // tm
    mod = pl.BlockSpec((1, r, d), lambda i: (i // per, 0, 0))
    vec = pl.BlockSpec((1, d), lambda i: (0, 0))
    return pl.pallas_call(
        _out_even_kernel,
        grid=(m // tm,),
        in_specs=[pl.BlockSpec((tm, ka), lambda i: (i, 0)),
                  pl.BlockSpec((tm, kb), lambda i: (i, 0)),
                  pl.BlockSpec((ka + kb, d), lambda i: (0, 0)),
                  pl.BlockSpec((tm, d), lambda i: (i, 0)),
                  mod, vec, vec, mod, mod],
        out_specs=[pl.BlockSpec((tm, d), lambda i: (i, 0)),
                   pl.BlockSpec((tm, d), lambda i: (i, 0))],
        out_shape=[jax.ShapeDtypeStruct((m, d), F32), jax.ShapeDtypeStruct((m, d), BF16)],
        compiler_params=_params("arbitrary"),
        name="out_proj_even",
    )(ya, yb, w, x, gt, g, b, sc, sh)


def _out_odd_kernel(a_ref, w_ref, x_ref, gt_ref, g_ref, b_ref, sc_ref, sh_ref, xo_ref, uo_ref, acc_ref):
    k = pl.program_id(1)

    @pl.when(k == 0)
    def _():
        acc_ref[...] = jnp.zeros_like(acc_ref)

    acc_ref[...] += _dot(a_ref[...], w_ref[...])

    @pl.when(k == pl.num_programs(1) - 1)
    def _():
        y = _post_norm(acc_ref[...], x_ref[...], gt_ref[0], g_ref[...], b_ref[...])
        xo_ref[...] = y
        uo_ref[...] = (y * (1.0 + sc_ref[0]) + sh_ref[0]).astype(uo_ref.dtype)


def _out_odd(a, w, x, gt, g, b, sc, sh, rows_per_mod, tm, tk):
    m, d = x.shape
    kdim = a.shape[1]
    r = gt.shape[1]
    per = rows_per_mod // tm
    mod = pl.BlockSpec((1, r, d), lambda i, k: (i // per, 0, 0))
    vec = pl.BlockSpec((1, d), lambda i, k: (0, 0))
    return pl.pallas_call(
        _out_odd_kernel,
        grid=(m // tm, kdim // tk),
        in_specs=[pl.BlockSpec((tm, tk), lambda i, k: (i, k)),
                  pl.BlockSpec((tk, d), lambda i, k: (k, 0)),
                  pl.BlockSpec((tm, d), lambda i, k: (i, 0)),
                  mod, vec, vec, mod, mod],
        out_specs=[pl.BlockSpec((tm, d), lambda i, k: (i, 0)),
                   pl.BlockSpec((tm, d), lambda i, k: (i, 0))],
        out_shape=[jax.ShapeDtypeStruct((m, d), F32), jax.ShapeDtypeStruct((m, d), BF16)],
        scratch_shapes=[pltpu.VMEM((tm, d), F32)],
        compiler_params=_params("arbitrary", "arbitrary"),
        name="out_proj_odd",
    )(a, w, x, gt, g, b, sc, sh)


CONV_A_HALO = 32
CONV_A_ROWS = 64


def _group_norm_gate(y, gate, g, b):
    outs = []
    for s in range(y.shape[1] // GROUP_A):
        sl = slice(s * GROUP_A, (s + 1) * GROUP_A)
        seg = y[:, sl]
        mu = jnp.mean(seg, axis=-1, keepdims=True)
        sc = seg - mu
        var = jnp.mean(sc * sc, axis=-1, keepdims=True)
        yn = sc * lax.rsqrt(var + LN_EPS) * g[:, sl] + b[:, sl]
        outs.append(_silu(yn) * _silu(gate[:, sl]))
    return outs[0] if len(outs) == 1 else jnp.concatenate(outs, axis=1)


def _conv_a_kernel(val_ref, glu_ref, gate_ref, w_ref, g_ref, b_ref, y_ref, st_ref, ext_ref):
    t = pl.program_id(2)
    tt = val_ref.shape[0]
    kw = w_ref.shape[0]
    off = CONV_A_HALO - (kw - 1)

    @pl.when(t == 0)
    def _():
        ext_ref[0:CONV_A_HALO, :] = jnp.zeros((CONV_A_HALO, ext_ref.shape[1]), F32)

    ext_ref[CONV_A_HALO:, :] = val_ref[...] * _sigmoid(glu_ref[...])
    sub = 8
    for r0 in range(0, tt, CONV_A_ROWS):
        acc = None
        for res in range(sub):
            part = None
            for j in range(kw):
                if (off + j) % sub == res:
                    a0 = r0 + off + j - res
                    term = w_ref[j:j + 1, :] * ext_ref[a0:a0 + CONV_A_ROWS + (sub if res else 0), :]
                    part = term if part is None else part + term
            if part is not None:
                part = part[res:res + CONV_A_ROWS, :]
                acc = part if acc is None else acc + part
        y = _group_norm_gate(acc, gate_ref[r0:r0 + CONV_A_ROWS, :], g_ref[...], b_ref[...])
        y_ref[r0:r0 + CONV_A_ROWS, :] = y.astype(y_ref.dtype)

    @pl.when(t == pl.num_programs(2) - 1)
    def _():
        st_ref[0] = ext_ref[tt + off:tt + CONV_A_HALO, :]

    ext_ref[0:CONV_A_HALO, :] = ext_ref[tt:tt + CONV_A_HALO, :]


def _conv_a_prompt(h, conv_w, gn_g, gn_b, batch, seq, tt=256, tc=256):
    nt = seq // tt
    nc = W_A // tc
    kw = conv_w.shape[0]
    col = lambda base: pl.BlockSpec((tt, tc), lambda b, c, t: (b * nt + t, base * nc + c))
    vec = pl.BlockSpec((1, tc), lambda b, c, t: (0, c))
    return pl.pallas_call(
        _conv_a_kernel,
        grid=(batch, nc, nt),
        in_specs=[col(0), col(1), col(2),
                  pl.BlockSpec((kw, tc), lambda b, c, t: (0, c)), vec, vec],
        out_specs=[pl.BlockSpec((tt, tc), lambda b, c, t: (b * nt + t, c)),
                   pl.BlockSpec((1, kw - 1, tc), lambda b, c, t: (b, 0, c))],
        out_shape=[jax.ShapeDtypeStruct((batch * seq, W_A), BF16),
                   jax.ShapeDtypeStruct((batch, kw - 1, W_A), F32)],
        scratch_shapes=[pltpu.VMEM((CONV_A_HALO + tt, tc), F32)],
        compiler_params=_params("arbitrary", "arbitrary", "arbitrary"),
        name="conv_a_prompt",
    )(h, h, h, conv_w, gn_g.reshape(1, -1), gn_b.reshape(1, -1))


def _conv_a_sample_kernel(h_ref, st_ref, w_ref, g_ref, b_ref, y_ref, so_ref):
    nseq, kw1, _ = st_ref.shape
    glu = h_ref[:, 0:W_A] * _sigmoid(h_ref[:, W_A:2 * W_A])
    rows = []
    for s in range(nseq):
        conv = jnp.sum(st_ref[s] * w_ref[0:kw1, :], axis=0, keepdims=True) + glu[s:s + 1, :] * w_ref[kw1:kw1 + 1, :]
        rows.append(conv)
        so_ref[s, 0:kw1 - 1, :] = st_ref[s, 1:kw1, :]
        so_ref[s, kw1 - 1:kw1, :] = glu[s:s + 1, :]
    rows.append(jnp.zeros((h_ref.shape[0] - nseq, W_A), F32))
    conv = jnp.concatenate(rows, axis=0)
    y = _group_norm_gate(conv, h_ref[:, 2 * W_A:3 * W_A], g_ref[...], b_ref[...])
    y_ref[...] = y.astype(y_ref.dtype)


def _conv_a_sample(h_s, state, conv_w, gn_g, gn_b):
    rows = h_s.shape[0]
    return pl.pallas_call(
        _conv_a_sample_kernel,
        out_shape=[jax.ShapeDtypeStruct((rows, W_A), BF16), jax.ShapeDtypeStruct(state.shape, F32)],
        compiler_params=pltpu.CompilerParams(vmem_limit_bytes=VMEM_LIMIT),
        name="conv_a_sample",
    )(h_s, state, conv_w, gn_g.reshape(1, -1), gn_b.reshape(1, -1))


ATTN_BLOCKS_PER_ITER = 8

def _attn_prompt_kernel(bias_ref, q_ref, k_ref, v_ref, gate_ref, o_ref, acc_ref, later_ref):
    h = pl.program_id(1)
    i = pl.program_id(2)
    blk = q_ref.shape[0]
    bias = bias_ref[h]
    q = (q_ref[...] * (SB_HEAD_DIM ** -0.5)).astype(BF16)
    row = lax.broadcasted_iota(jnp.int32, (blk, blk), 0)
    col = lax.broadcasted_iota(jnp.int32, (blk, blk), 1)
    after = jnp.where(row > col, 1.0, 0.0).astype(BF16)
    visible = col < row

    acc_ref[...] = jnp.zeros_like(acc_ref)
    later_ref[...] = jnp.zeros_like(later_ref)

    def blocks(js, masked):
        zs, sps = [], []
        for j in js:
            start = pl.multiple_of(j * blk, blk)
            z = _dot(q, k_ref[pl.ds(start, blk), :].astype(BF16), NT) + bias
            sp = _softplus(z)
            if masked:
                sp = jnp.where(visible, sp, 0.0)
            zs.append(z)
            sps.append(sp)
        later = later_ref[...]
        out = None
        for j, z, sp in zip(js, zs, sps):
            start = pl.multiple_of(j * blk, blk)
            inside = _dot(sp.astype(BF16), after)
            w = jnp.exp(z - sp - inside - later)
            if masked:
                w = jnp.where(visible, w, 0.0)
            pv = _dot(w.astype(BF16), v_ref[pl.ds(start, blk), :].astype(BF16))
            out = pv if out is None else out + pv
            later = later + jnp.sum(sp, axis=-1, keepdims=True)
        acc_ref[...] += out
        later_ref[...] = later

    blocks([i], True)

    def body(jj, carry):
        j = i - 1 - ATTN_BLOCKS_PER_ITER * jj
        blocks([j - t for t in range(ATTN_BLOCKS_PER_ITER)], False)
        return carry

    lax.fori_loop(0, i // ATTN_BLOCKS_PER_ITER, body, 0)
    rem = i % ATTN_BLOCKS_PER_ITER
    size = ATTN_BLOCKS_PER_ITER // 2
    while size >= 1:
        @pl.when((rem // size) % 2 == 1)
        def _(size=size):
            top = rem % (2 * size)
            blocks([top - 1 - t for t in range(size)], False)

        size //= 2

    o_ref[...] = (acc_ref[...] * _silu(gate_ref[...])).astype(o_ref.dtype)


def _attn_prompt(h, sb_bias, batch, seq, blk=256):
    nq = seq // blk
    base = 3 * W_A // SB_HEAD_DIM
    return pl.pallas_call(
        _attn_prompt_kernel,
        grid_spec=pltpu.PrefetchScalarGridSpec(
            num_scalar_prefetch=0,
            grid=(batch, SB_HEADS, nq),
            in_specs=[pl.BlockSpec(memory_space=pltpu.SMEM),
                      pl.BlockSpec((blk, SB_HEAD_DIM), lambda b, hh, i: (b * nq + i, base + hh)),
                      pl.BlockSpec((seq, SB_HEAD_DIM), lambda b, hh, i: (b, base + SB_HEADS + hh)),
                      pl.BlockSpec((seq, SB_HEAD_DIM), lambda b, hh, i: (b, base + 2 * SB_HEADS + hh)),
                      pl.BlockSpec((blk, SB_HEAD_DIM), lambda b, hh, i: (b * nq + i, base + 3 * SB_HEADS + hh))],
            out_specs=pl.BlockSpec((blk, SB_HEAD_DIM), lambda b, hh, i: (b * nq + i, hh)),
            scratch_shapes=[pltpu.VMEM((blk, SB_HEAD_DIM), F32), pltpu.VMEM((blk, 1), F32)]),
        out_shape=jax.ShapeDtypeStruct((batch * seq, W_B), BF16),
        compiler_params=_params("arbitrary", "arbitrary", "arbitrary"),
        name="attn_prompt",
    )(sb_bias, h, h, h, h)


PAGES_PER_STEP = 8


def _attn_sample_kernel(pt_ref, q_ref, bias_ref, gate_ref, own_ref, tok_ref, tokt_ref, *refs):
    npg = PAGES_PER_STEP
    k_refs, v_refs = refs[:npg], refs[npg:2 * npg]
    o_ref, acc_ref, later_ref = refs[2 * npg:]
    p = pl.program_id(1)
    rows = q_ref.shape[1]
    page = k_refs[0].shape[2] // SB_HEADS

    @pl.when(p == 0)
    def _():
        acc_ref[...] = jnp.zeros_like(acc_ref)
        later_ref[...] = jnp.zeros_like(later_ref)

    q = (q_ref[0] * (SB_HEAD_DIM ** -0.5)).astype(BF16)
    own = own_ref[...]
    r2 = lax.broadcasted_iota(jnp.int32, (page, page), 0)
    c2 = lax.broadcasted_iota(jnp.int32, (page, page), 1)
    after = jnp.where(r2 > c2, 1.0, 0.0).astype(BF16)

    def two_pass(x, m):
        hi, lo = _split(x)
        both = _dot(jnp.concatenate([hi, lo], axis=0), m)
        return both[0:rows, :] + both[rows:, :]

    pages = range(npg)
    zs = [two_pass(_dot(q, k_refs[r][0, 0].astype(BF16), NT) * own, tok_ref[...]) + bias_ref[...] for r in pages]
    sps = [_softplus(z) for z in zs]
    inside = [_dot(sp.astype(BF16), after) for sp in sps]
    later = later_ref[...]
    ws = []
    for r in pages:
        ws.append(jnp.exp(zs[r] - sps[r] - inside[r] - later).astype(BF16))
        later = later + jnp.sum(sps[r], axis=-1, keepdims=True)
    spread = [(_dot(w, tokt_ref[...]) * own).astype(BF16) for w in ws]
    out = acc_ref[...]
    for r in pages:
        out = out + _dot(spread[r], v_refs[r][0, 0].astype(BF16))
    acc_ref[...] = out
    later_ref[...] = later

    @pl.when(p == pl.num_programs(1) - 1)
    def _():
        o_ref[0] = acc_ref[...] * _silu(gate_ref[0])


def _attn_sample(q, gate, bias, cache_k, cache_v, page_table, layer):
    nseq, rows, _ = q.shape
    n_pages = page_table.shape[1]
    npg = PAGES_PER_STEP
    blk = cache_k.shape[2]

    def page_spec(r):
        return pl.BlockSpec((1, 1, blk, SB_HEAD_DIM),
                            lambda s, p, pt: (layer, pt[s, n_pages - 1 - (p * npg + r)], 0, 0))

    row_spec = pl.BlockSpec((1, rows, SB_HEAD_DIM), lambda s, p, pt: (s, 0, 0))
    page = blk // SB_HEADS
    r_head = jnp.arange(blk, dtype=jnp.int32) % SB_HEADS
    r_tok = jnp.arange(blk, dtype=jnp.int32) // SB_HEADS
    own = (jnp.arange(rows, dtype=jnp.int32)[:, None] == r_head[None, :]).astype(F32)
    tok = (r_tok[:, None] == jnp.arange(page, dtype=jnp.int32)[None, :]).astype(BF16)
    whole = lambda shape: pl.BlockSpec(shape, lambda s, p, pt: (0, 0))
    return pl.pallas_call(
        _attn_sample_kernel,
        grid_spec=pltpu.PrefetchScalarGridSpec(
            num_scalar_prefetch=1,
            grid=(nseq, n_pages // npg),
            in_specs=[row_spec, whole((rows, 1)), row_spec, whole((rows, blk)), whole((blk, page)), whole((page, blk))]
                     + [page_spec(r) for r in range(npg)] + [page_spec(r) for r in range(npg)],
            out_specs=row_spec,
            scratch_shapes=[pltpu.VMEM((rows, SB_HEAD_DIM), F32), pltpu.VMEM((rows, 1), F32)]),
        out_shape=jax.ShapeDtypeStruct((nseq, rows, SB_HEAD_DIM), F32),
        compiler_params=_params("arbitrary", "arbitrary"),
        name="attn_sample",
    )(page_table, q, bias, gate, own, tok, tok.T, *([cache_k] * npg), *([cache_v] * npg))


def _gdn_gate_kernel(u_ref, wt_ref, alog_ref, dt_ref, beta_ref, gc_ref):
    raw = _dot(wt_ref[...].astype(BF16), u_ref[...], NT)
    hv = raw.shape[0] // 2
    beta_ref[0] = _sigmoid(raw[0:hv, :])
    g = -jnp.exp(alog_ref[...]) * _softplus(raw[hv:, :] + dt_ref[...])
    r2 = lax.broadcasted_iota(jnp.int32, (GDN_CHUNK, GDN_CHUNK), 0)
    c2 = lax.broadcasted_iota(jnp.int32, (GDN_CHUNK, GDN_CHUNK), 1)
    upto = jnp.where(r2 <= c2, 1.0, 0.0).astype(BF16)
    gh, gl = _split(g)
    gl2 = (g - gh.astype(F32) - gl.astype(F32)).astype(BF16)
    for c in range(g.shape[1] // GDN_CHUNK):
        sl = slice(c * GDN_CHUNK, (c + 1) * GDN_CHUNK)
        gc_ref[0, :, sl] = _dot(gh[:, sl], upto) + (_dot(gl[:, sl], upto) + _dot(gl2[:, sl], upto))


def _gdn_gates(u, w_ba_t, a_log, dt_bias, batch, seq, tt=512):
    nt = seq // tt
    hv2, d = w_ba_t.shape
    hv = hv2 // 2
    out = pl.BlockSpec((1, hv, tt), lambda b, t: (b, 0, t))
    colv = pl.BlockSpec((hv, 1), lambda b, t: (0, 0))
    return pl.pallas_call(
        _gdn_gate_kernel,
        grid=(batch, nt),
        in_specs=[pl.BlockSpec((tt, d), lambda b, t: (b * nt + t, 0)),
                  pl.BlockSpec((hv2, d), lambda b, t: (0, 0)), colv, colv],
        out_specs=[out, out],
        out_shape=[jax.ShapeDtypeStruct((batch, hv, seq), F32)] * 2,
        compiler_params=_params("arbitrary", "arbitrary"),
        name="gdn_gates",
    )(u, w_ba_t, a_log.reshape(hv, 1), dt_bias.reshape(hv, 1))


CONV_C_HALO = 8


def _qk_norm(y, cblk, tc):
    first = cblk * tc
    is_q = first < GDN_QK_W
    is_qk = first < 2 * GDN_QK_W
    outs = []
    for s in range(tc // GDN_HEAD_DIM):
        seg = y[:, s * GDN_HEAD_DIM:(s + 1) * GDN_HEAD_DIM]
        rs = lax.rsqrt(jnp.sum(seg * seg, axis=-1, keepdims=True) + RMS_EPS)
        f = jnp.where(is_q, rs * (GDN_HEAD_DIM ** -0.5), jnp.where(is_qk, rs, 1.0))
        outs.append(seg * f)
    return outs[0] if len(outs) == 1 else jnp.concatenate(outs, axis=1)


CONV_C_ROWS = 128


def _in_proj_conv_kernel(a_ref, as_ref, w_ref, cw_ref, y_ref, os_ref, st_ref, wb_ref, ext_ref, *, dims, tiles_per_seq):
    j = pl.program_id(0)
    i = pl.program_id(1)
    tm, tn = y_ref.shape
    kw = cw_ref.shape[0]
    off = CONV_C_HALO - (kw - 1)

    @pl.when(i == 0)
    def _():
        wb_ref[...] = w_ref[0].astype(BF16)
        os_ref[...] = _dot(as_ref[...], wb_ref[...], dims)

    @pl.when(i % tiles_per_seq == 0)
    def _():
        ext_ref[0:CONV_C_HALO, :] = jnp.zeros((CONV_C_HALO, tn), F32)

    @pl.when(i % tiles_per_seq != 0)
    def _():
        ext_ref[0:CONV_C_HALO, :] = ext_ref[tm:tm + CONV_C_HALO, :]

    ext_ref[CONV_C_HALO:, :] = _dot(a_ref[...], wb_ref[...], dims)
    st_ref[0] = ext_ref[tm + off:tm + CONV_C_HALO, :]
    for r0 in range(0, tm, CONV_C_ROWS):
        acc = cw_ref[0:1, :] * ext_ref[r0 + off:r0 + off + CONV_C_ROWS, :]
        for t in range(1, kw):
            acc = acc + cw_ref[t:t + 1, :] * ext_ref[r0 + off + t:r0 + off + t + CONV_C_ROWS, :]
        y_ref[r0:r0 + CONV_C_ROWS, :] = _qk_norm(_silu(acc), j, tn)


def _in_proj_conv(a, a_s, w, layer, n_out, conv_w, batch, seq, tm, tn, transposed):
    m, k = a.shape
    r = a_s.shape[0]
    kw = conv_w.shape[0]
    assert m % tm == 0 and n_out % tn == 0 and seq % tm == 0 and tm % CONV_C_ROWS == 0
    tiles_per_seq = seq // tm
    w_spec, wb_shape, dims = _weight_spec(k, tn, layer, 0, transposed)
    return pl.pallas_call(
        functools.partial(_in_proj_conv_kernel, dims=dims, tiles_per_seq=tiles_per_seq),
        grid=(n_out // tn, m // tm),
        in_specs=[pl.BlockSpec((tm, k), lambda j, i: (i, 0)),
                  pl.BlockSpec((r, k), lambda j, i: (0, 0)),
                  w_spec,
                  pl.BlockSpec((kw, tn), lambda j, i: (0, j))],
        out_specs=[pl.BlockSpec((tm, tn), lambda j, i: (i, j)),
                   pl.BlockSpec((r, tn), lambda j, i: (0, j)),
                   pl.BlockSpec((1, kw - 1, tn), lambda j, i: (i // tiles_per_seq, 0, j))],
        out_shape=[jax.ShapeDtypeStruct((m, n_out), F32), jax.ShapeDtypeStruct((r, n_out), F32),
                   jax.ShapeDtypeStruct((batch, kw - 1, n_out), F32)],
        scratch_shapes=[pltpu.VMEM(wb_shape, BF16), pltpu.VMEM((CONV_C_HALO + tm, tn), F32)],
        compiler_params=_params("arbitrary", "arbitrary"),
        name="in_proj_conv",
    )(a, a_s, w, conv_w)


def _gdn_prompt_kernel(q_ref, k_ref, v_ref, z_ref, beta_ref, gc_ref, gw_ref, o_ref, s_out_ref, s_ref):
    t = pl.program_id(2)
    c = GDN_CHUNK
    tt = q_ref.shape[0]
    rep = v_ref.shape[1] // GDN_HEAD_DIM

    @pl.when(t == 0)
    def _():
        s_ref[...] = jnp.zeros_like(s_ref)

    row = lax.broadcasted_iota(jnp.int32, (c, c), 0)
    col = lax.broadcasted_iota(jnp.int32, (c, c), 1)
    incl = row >= col
    strict = row > col
    eye = jnp.where(row == col, 1.0, 0.0)
    merge = []
    m = 1
    while m < c:
        bi, bj = row // m, col // m
        merge.append(jnp.logical_and(bi - bj == 1, bj % 2 == 0))
        m *= 2

    bf = lambda x: x.astype(BF16)
    nchunk = tt // c
    units = [(n, hv) for n in range(nchunk) for hv in range(rep)]
    rows_of = lambda n: slice(n * c, (n + 1) * c)
    cols_of = lambda hv: slice(hv * GDN_HEAD_DIM, (hv + 1) * GDN_HEAD_DIM)

    kq, kbf = {}, {}
    for n in range(nchunk):
        kbf[n] = bf(k_ref[rows_of(n), :])
        kq[n] = _dot(jnp.concatenate([kbf[n], bf(q_ref[rows_of(n), :])], axis=0), kbf[n], NT)
    spread = lambda r: jnp.broadcast_to(r, (c, c))
    b_row, be_row, b_col, be_col, e_col, kd_col, e_last, lneg, intra, xo = {}, {}, {}, {}, {}, {}, {}, {}, {}, {}
    for un in units:
        n, hv = un
        gr = gc_ref[0, hv, :, rows_of(n)]
        br = beta_ref[0, hv, :, rows_of(n)]
        g_last = gr[:, c - 1:c]
        er = jnp.exp(gr)
        g_row = spread(gr)
        b_row[un] = spread(br)
        be_row[un] = spread(br * er)
        b_col[un] = b_row[un].T
        be_col[un] = be_row[un].T
        e_col[un] = spread(er).T
        kd_col[un] = spread(jnp.exp(g_last - gr)).T
        e_last[un] = jnp.exp(g_last)
        decay = jnp.exp(g_row.T - g_row)
        lneg[un] = jnp.where(strict, (-kq[n][0:c, :]) * b_col[un] * decay, 0.0)
        intra[un] = bf(jnp.where(incl, kq[n][c:, :] * decay, 0.0))
        xo[un] = jnp.where(merge[0], lneg[un], 0.0)
    for lvl in merge[1:]:
        for un in units:
            cn = jnp.where(lvl, lneg[un], 0.0)
            y = cn + _dot(bf(xo[un]), bf(cn))
            xo[un] = xo[un] + (y + _dot(bf(y), bf(xo[un])))
    u_val, wq, k_dec_t = {}, {}, {}
    for un in units:
        n, hv = un
        k = k_ref[rows_of(n), :]
        v = v_ref[rows_of(n), cols_of(hv)]
        u_val[un] = v * b_col[un] + _dot(bf(xo[un] * b_row[un]), bf(v))
        w_key = k * be_col[un] + _dot(bf(xo[un] * be_row[un]), kbf[n])
        wq[un] = bf(jnp.concatenate([w_key, q_ref[rows_of(n), :] * e_col[un]], axis=0))
        k_dec_t[un] = bf((k * kd_col[un]).T)

    for un in units:
        n, hv = un
        s = s_ref[hv]
        ws = _dot(wq[un], bf(s))
        v_new = bf(u_val[un] - ws[0:c, :])
        o = ws[c:, :] + _dot(intra[un], v_new)
        s_ref[hv] = s * e_last[un] + _dot(k_dec_t[un], v_new)
        o = o * lax.rsqrt(jnp.mean(o * o, axis=-1, keepdims=True) + RMS_EPS) * gw_ref[...]
        o_ref[rows_of(n), cols_of(hv)] = (o * _silu(z_ref[rows_of(n), cols_of(hv)])).astype(o_ref.dtype)

    @pl.when(t == pl.num_programs(2) - 1)
    def _():
        s_out_ref[0] = s_ref[...]


def _gdn_prompt(qkv, h, z_col0, beta, gc, gnorm_w, batch, seq, tt=1024):
    tt = min(tt, seq)
    assert seq % tt == 0 and tt % GDN_CHUNK == 0
    nt = seq // tt
    rep = GDN_V_HEADS // GDN_QK_HEADS
    dh = GDN_HEAD_DIM
    vw = rep * dh
    gate = pl.BlockSpec((1, rep, 1, tt), lambda b, hq, t: (b, hq, 0, t))
    return pl.pallas_call(
        _gdn_prompt_kernel,
        grid=(batch, GDN_QK_HEADS, nt),
        in_specs=[pl.BlockSpec((tt, dh), lambda b, hq, t: (b * nt + t, hq)),
                  pl.BlockSpec((tt, dh), lambda b, hq, t: (b * nt + t, GDN_QK_HEADS + hq)),
                  pl.BlockSpec((tt, vw), lambda b, hq, t: (b * nt + t, 2 * GDN_QK_W // vw + hq)),
                  pl.BlockSpec((tt, vw), lambda b, hq, t: (b * nt + t, z_col0 // vw + hq)),
                  gate, gate,
                  pl.BlockSpec((1, dh), lambda b, hq, t: (0, 0))],
        out_specs=[pl.BlockSpec((tt, vw), lambda b, hq, t: (b * nt + t, hq)),
                   pl.BlockSpec((1, rep, dh, dh), lambda b, hq, t: (b, hq, 0, 0))],
        out_shape=[jax.ShapeDtypeStruct((batch * seq, GDN_V_W), BF16),
                   jax.ShapeDtypeStruct((batch, GDN_V_HEADS, dh, dh), F32)],
        scratch_shapes=[pltpu.VMEM((rep, dh, dh), F32)],
        compiler_params=_params("arbitrary", "arbitrary", "arbitrary"),
        name="gdn_prompt",
    )(qkv, qkv, qkv, h, beta, gc, gnorm_w.reshape(1, dh))


def _conv_c_sample_kernel(h_ref, st_ref, w_ref, u_ref, wba_ref, alog_ref, dt_ref, y_ref, so_ref, beta_ref, eg_ref):
    nseq, kw1, width = st_ref.shape
    tc = 1024
    for c0 in range(0, width, tc):
        cs = slice(c0, c0 + tc)
        rows = []
        for s in range(nseq):
            x_new = h_ref[s:s + 1, cs]
            conv = jnp.sum(st_ref[s, :, cs] * w_ref[0:kw1, cs], axis=0, keepdims=True) + x_new * w_ref[kw1:kw1 + 1, cs]
            rows.append(conv)
            so_ref[s, 0:kw1 - 1, cs] = st_ref[s, 1:kw1, cs]
            so_ref[s, kw1 - 1:kw1, cs] = x_new
        rows.append(jnp.zeros((h_ref.shape[0] - nseq, tc), F32))
        y_ref[:, cs] = _qk_norm(_silu(jnp.concatenate(rows, axis=0)), c0 // tc, tc)
    raw = _dot(u_ref[...], wba_ref[...].astype(BF16), NT)
    hv = raw.shape[1] // 2
    beta_ref[...] = _sigmoid(raw[:, 0:hv])
    eg_ref[...] = jnp.exp(-jnp.exp(alog_ref[...]) * _softplus(raw[:, hv:] + dt_ref[...]))


def _conv_c_sample(h_s, state, conv_w, u_s, w_ba, a_log, dt_bias):
    rows = h_s.shape[0]
    hv = a_log.shape[0]
    return pl.pallas_call(
        _conv_c_sample_kernel,
        out_shape=[jax.ShapeDtypeStruct((rows, GDN_CONV_DIM), F32), jax.ShapeDtypeStruct(state.shape, F32),
                   jax.ShapeDtypeStruct((rows, hv), F32), jax.ShapeDtypeStruct((rows, hv), F32)],
        compiler_params=pltpu.CompilerParams(vmem_limit_bytes=VMEM_LIMIT),
        name="conv_c_sample",
    )(h_s, state, conv_w, u_s, w_ba, a_log.reshape(1, hv), dt_bias.reshape(1, hv))


def _gdn_sample_kernel(beta_ref, eg_ref, qc_ref, kc_ref, v_ref, z_ref, gw_ref, s_ref, o_ref, so_ref):
    sq = pl.program_id(0)
    rep = GDN_V_HEADS // GDN_QK_HEADS
    for hv in range(GDN_V_HEADS):
        hq = hv // rep
        k_col = kc_ref[0, :, hq:hq + 1]
        q_col = qc_ref[0, :, hq:hq + 1]
        s = s_ref[0, hv] * eg_ref[sq, hv]
        delta = (v_ref[0, hv:hv + 1, :] - jnp.sum(s * k_col, axis=0, keepdims=True)) * beta_ref[sq, hv]
        s = s + k_col * delta
        so_ref[0, hv] = s
        o = jnp.sum(s * q_col, axis=0, keepdims=True)
        o = o * lax.rsqrt(jnp.mean(o * o, axis=-1, keepdims=True) + RMS_EPS) * gw_ref[...]
        o_ref[0, hv:hv + 1, :] = o * _silu(z_ref[0, hv:hv + 1, :])


def _gdn_sample(beta, eg, q_col, k_col, v, z, gnorm_w, state):
    nseq, hv, dk, dv = state.shape
    hq = q_col.shape[2]
    smem = pl.BlockSpec(memory_space=pltpu.SMEM)
    colspec = pl.BlockSpec((1, dk, hq), lambda s: (s, 0, 0))
    rowspec = pl.BlockSpec((1, hv, dv), lambda s: (s, 0, 0))
    stspec = pl.BlockSpec((1, hv, dk, dv), lambda s: (s, 0, 0, 0))
    return pl.pallas_call(
        _gdn_sample_kernel,
        grid=(nseq,),
        in_specs=[smem, smem, colspec, colspec, rowspec, rowspec,
                  pl.BlockSpec((1, dv), lambda s: (0, 0)), stspec],
        out_specs=[rowspec, stspec],
        out_shape=[jax.ShapeDtypeStruct((nseq, hv, dv), F32), jax.ShapeDtypeStruct(state.shape, F32)],
        compiler_params=_params("arbitrary"),
        name="gdn_sample",
    )(beta, eg, q_col, k_col, v, z, gnorm_w.reshape(1, dv), state)


def kernel(x_prompt, x_sample, c_prompt, c_sample, cache_k, cache_v, page_table, state_conv_a, state_conv_c, state_delta, w_ada, b_ada, ln_g, ln_b, w_in_even, conv_w_a, gn_g_a, gn_b_a, sb_bias, w_out_even, w_in_odd, conv_w_c, a_log_c, dt_bias_c, gnorm_w_c, w_out_odd):
    batch, seq, d = x_prompt.shape
    nseq = x_sample.shape[0]
    n_even, n_pool, page_size = cache_k.shape[:3]
    np_rows = batch * seq
    pad = SAMPLE_ROWS - nseq

    c_all = jnp.concatenate([c_prompt, c_sample, jnp.zeros((SAMPLE_ROWS - batch - nseq, d), F32)], axis=0)
    mod = _ada_all(c_all, w_ada, b_ada)

    def mods(layer):
        m = mod[layer]
        parts = [m[:, i * d:(i + 1) * d] for i in range(3)]
        pm = [p[0:batch].reshape(batch, 1, d) for p in parts]
        sm = [jnp.pad(p[batch:batch + nseq], ((0, pad), (0, 0))).reshape(1, SAMPLE_ROWS, d) for p in parts]
        return pm, sm

    xp = x_prompt.reshape(np_rows, d)
    xs = jnp.pad(x_sample.reshape(nseq, d), ((0, pad), (0, 0)))
    (sh_p, sc_p, _), (sh_s, sc_s, _) = mods(0)
    up = _modulate(xp, sc_p, sh_p, seq, 512)
    us = _modulate(xs, sc_s, sh_s, SAMPLE_ROWS, SAMPLE_ROWS)

    w_in_odd_t = jnp.swapaxes(w_in_odd, 1, 2)
    ck = cache_k.reshape(n_even, n_pool, page_size * SB_HEADS, SB_HEAD_DIM)
    cv = cache_v.reshape(n_even, n_pool, page_size * SB_HEADS, SB_HEAD_DIM)
    bias_pad = lambda b: jnp.pad(b, (0, SAMPLE_ROWS - SB_HEADS)).reshape(SAMPLE_ROWS, 1)
    heads_pad = lambda a: jnp.pad(a.reshape(nseq, SB_HEADS, SB_HEAD_DIM), ((0, 0), (0, SAMPLE_ROWS - SB_HEADS), (0, 0)))

    kp_l, vp_l, ks_l, vs_l = [], [], [], []
    cap_l, cas_l, ccp_l, ccs_l, dp_l, ds_l = [], [], [], [], [], []
    for layer in range(DEPTH):
        j = layer // 2
        (_, _, gt_p), (_, _, gt_s) = mods(layer)
        if layer + 1 < DEPTH:
            (shn_p, scn_p, _), (shn_s, scn_s, _) = mods(layer + 1)
        else:
            shn_p = scn_p = jnp.zeros((batch, 1, d), F32)
            shn_s = scn_s = jnp.zeros((1, SAMPLE_ROWS, d), F32)
        g_ln = ln_g[layer].reshape(1, d)
        b_ln = ln_b[layer].reshape(1, d)
        if layer % 2 == 0:
            w_out = w_out_even[j].astype(BF16)
            hp, hs = _in_proj(up, us, w_in_even, j, 0, w_in_even.shape[2], min(1024, np_rows), 1024)
            ya_p, buf_p = _conv_a_prompt(hp, conv_w_a[j], gn_g_a[j], gn_b_a[j], batch, seq)
            yb_p = _attn_prompt(hp, sb_bias[j], batch, seq)
            xp, up = _out_even(ya_p, yb_p, w_out, xp, gt_p, g_ln, b_ln, scn_p, shn_p, seq, 256)
            kp_l.append(hp[:, 3 * W_A + W_B:3 * W_A + 2 * W_B].reshape(batch, seq, SB_HEADS, SB_HEAD_DIM))
            vp_l.append(hp[:, 3 * W_A + 2 * W_B:3 * W_A + 3 * W_B].reshape(batch, seq, SB_HEADS, SB_HEAD_DIM))
            cap_l.append(buf_p)
            ya_s, buf_s = _conv_a_sample(hs, state_conv_a[j], conv_w_a[j], gn_g_a[j], gn_b_a[j])
            q_s = heads_pad(hs[:nseq, 3 * W_A:3 * W_A + W_B])
            gate_s = heads_pad(hs[:nseq, 3 * W_A + 3 * W_B:3 * W_A + 4 * W_B])
            o_s = _attn_sample(q_s, gate_s, bias_pad(sb_bias[j]), ck, cv, page_table, j)
            yb_s = jnp.pad(o_s[:, :SB_HEADS].reshape(nseq, W_B), ((0, pad), (0, 0))).astype(BF16)
            xs, us = _out_even(ya_s, yb_s, w_out, xs, gt_s, g_ln, b_ln, scn_s, shn_s, SAMPLE_ROWS, SAMPLE_ROWS)
            ks_l.append(hs[:nseq, 3 * W_A + W_B:3 * W_A + 2 * W_B].reshape(nseq, 1, SB_HEADS, SB_HEAD_DIM))
            vs_l.append(hs[:nseq, 3 * W_A + 2 * W_B:3 * W_A + 3 * W_B].reshape(nseq, 1, SB_HEADS, SB_HEAD_DIM))
            cas_l.append(buf_s)
        else:
            w_out = w_out_odd[j].astype(BF16)
            n_main = GDN_CONV_DIM + GDN_V_W
            w_ba_t = w_in_odd_t[j, n_main:, :]
            tm = min(1024, seq)
            qkv_p, qkv_s_raw, buf_p = _in_proj_conv(up, us, w_in_odd_t, j, GDN_CONV_DIM, conv_w_c[j], batch, seq,
                                                    tm, 1024, True)
            z_p, z_s_raw = _in_proj(up, us, w_in_odd_t, j, GDN_CONV_DIM // 1024, GDN_V_W, tm, 1024, True)
            hs = jnp.concatenate([qkv_s_raw, z_s_raw], axis=1)
            beta_p, gc_p = _gdn_gates(up, w_ba_t, a_log_c[j], dt_bias_c[j], batch, seq)
            o_p, s_p = _gdn_prompt(qkv_p, z_p, 0, beta_p.reshape(batch, GDN_V_HEADS, 1, seq),
                                   gc_p.reshape(batch, GDN_V_HEADS, 1, seq), gnorm_w_c[j], batch, seq)
            xp, up = _out_odd(o_p, w_out, xp, gt_p, g_ln, b_ln, scn_p, shn_p, seq, 512, 1024)
            ccp_l.append(buf_p)
            dp_l.append(s_p)
            qkv_s, buf_s, beta_s, eg_s = _conv_c_sample(hs, state_conv_c[j], conv_w_c[j], us, w_ba_t,
                                                        a_log_c[j], dt_bias_c[j])
            heads_col = lambda a: jnp.swapaxes(a.reshape(nseq, GDN_QK_HEADS, GDN_HEAD_DIM), 1, 2)
            q_col = heads_col(qkv_s[:nseq, 0:GDN_QK_W])
            k_col = heads_col(qkv_s[:nseq, GDN_QK_W:2 * GDN_QK_W])
            v_s = qkv_s[:nseq, 2 * GDN_QK_W:].reshape(nseq, GDN_V_HEADS, GDN_HEAD_DIM)
            z_s = hs[:nseq, GDN_CONV_DIM:n_main].reshape(nseq, GDN_V_HEADS, GDN_HEAD_DIM)
            o_s, s_s = _gdn_sample(beta_s[:nseq], eg_s[:nseq], q_col, k_col, v_s, z_s, gnorm_w_c[j], state_delta[j])
            a_s = jnp.pad(o_s.reshape(nseq, GDN_V_W), ((0, pad), (0, 0))).astype(BF16)
            xs, us = _out_odd(a_s, w_out, xs, gt_s, g_ln, b_ln, scn_s, shn_s, SAMPLE_ROWS, SAMPLE_ROWS, 1024)
            ccs_l.append(buf_s)
            ds_l.append(s_s)
    return (xp.reshape(batch, seq, d), xs[:nseq].reshape(nseq, 1, d),
            jnp.stack(kp_l), jnp.stack(vp_l), jnp.stack(ks_l), jnp.stack(vs_l),
            jnp.stack(cap_l), jnp.stack(cas_l), jnp.stack(ccp_l), jnp.stack(ccs_l),
            jnp.stack(dp_l), jnp.stack(ds_l))
```

```python
import functools

import jax
import jax.numpy as jnp
from jax import lax
from jax.experimental import pallas as pl
from jax.experimental.pallas import tpu as pltpu

F32 = jnp.float32
BF16 = jnp.bfloat16

D_MODEL = 2048
DEPTH = 4
CONV_WIDTH_A = 31
W_A = 1024
GROUP_A = 128
SB_HEAD_DIM = 128
SB_HEADS = 8
W_B = 1024
GDN_HEAD_DIM = 128
GDN_QK_HEADS = 16
GDN_V_HEADS = 32
GDN_QK_W = 2048
GDN_V_W = 4096
GDN_CONV_WIDTH = 4
GDN_CONV_DIM = 8192
GDN_CHUNK = 128
DEEPNORM_ALPHA = (2 * DEPTH) ** 0.25
LN_EPS = 1e-5
RMS_EPS = 1e-6
SAMPLE_ROWS = 16
LANES = 128
VMEM_LIMIT = 48 * 1024 * 1024

NN = (((1,), (0,)), ((), ()))
NT = (((1,), (1,)), ((), ()))


def _params(*sem):
    return pltpu.CompilerParams(dimension_semantics=sem, vmem_limit_bytes=VMEM_LIMIT)


def _dot(a, b, dims=NN):
    return lax.dot_general(a, b, dims, preferred_element_type=F32)


def _split(x):
    hi = x.astype(BF16)
    lo = (x - hi.astype(F32)).astype(BF16)
    return hi, lo


def _dot3(a, b, dims=NN):
    ah, al = _split(a)
    bh, bl = _split(b)
    return _dot(ah, bh, dims) + (_dot(ah, bl, dims) + _dot(al, bh, dims))


def _dot_exact_rhs(a, m_bf16):
    ah, al = _split(a)
    return _dot(ah, m_bf16) + _dot(al, m_bf16)


def _sigmoid(x):
    return 1.0 / (1.0 + jnp.exp(-x))


def _silu(x):
    return x * _sigmoid(x)


def _softplus(x):
    return jnp.maximum(x, 0.0) + jnp.log(1.0 + jnp.exp(-jnp.abs(x)))


def _ada_kernel(c_ref, w_ref, b_ref, o_ref):
    s = _silu(c_ref[...]).astype(BF16)
    o_ref[0] = _dot(s, w_ref[0].astype(BF16)) + b_ref[0]


def _ada_all(c_all, w_ada, b_ada):
    depth, d, n = w_ada.shape
    rows = c_all.shape[0]
    tn = 512
    return pl.pallas_call(
        _ada_kernel,
        grid=(depth, n // tn),
        in_specs=[pl.BlockSpec((rows, d), lambda l, j: (0, 0)),
                  pl.BlockSpec((1, d, tn), lambda l, j: (l, 0, j)),
                  pl.BlockSpec((1, 1, tn), lambda l, j: (l, 0, j))],
        out_specs=pl.BlockSpec((1, rows, tn), lambda l, j: (l, 0, j)),
        out_shape=jax.ShapeDtypeStruct((depth, rows, n), F32),
        compiler_params=_params("arbitrary", "arbitrary"),
        name="ada_mod",
    )(c_all, w_ada, b_ada.reshape(depth, 1, n))


def _modulate_kernel(x_ref, sc_ref, sh_ref, u_ref):
    u_ref[...] = (x_ref[...] * (1.0 + sc_ref[0]) + sh_ref[0]).astype(u_ref.dtype)


def _modulate(x, sc, sh, rows_per_mod, tm):
    m, d = x.shape
    r = sc.shape[1]
    per = rows_per_mod // tm
    return pl.pallas_call(
        _modulate_kernel,
        grid=(m // tm,),
        in_specs=[pl.BlockSpec((tm, d), lambda i: (i, 0)),
                  pl.BlockSpec((1, r, d), lambda i: (i // per, 0, 0)),
                  pl.BlockSpec((1, r, d), lambda i: (i // per, 0, 0))],
        out_specs=pl.BlockSpec((tm, d), lambda i: (i, 0)),
        out_shape=jax.ShapeDtypeStruct((m, d), BF16),
        compiler_params=_params("arbitrary"),
        name="modulate",
    )(x, sc, sh)


def _in_proj_kernel(a_ref, as_ref, w_ref, o_ref, os_ref, wb_ref, *, dims):
    @pl.when(pl.program_id(1) == 0)
    def _():
        wb_ref[...] = w_ref[0].astype(BF16)
        os_ref[...] = _dot(as_ref[...], wb_ref[...], dims)

    o_ref[...] = _dot(a_ref[...], wb_ref[...], dims)


def _weight_spec(k, tn, layer, col0, transposed):
    if transposed:
        return pl.BlockSpec((1, tn, k), lambda j, i: (layer, col0 + j, 0)), (tn, k), NT
    return pl.BlockSpec((1, k, tn), lambda j, i: (layer, 0, col0 + j)), (k, tn), NN


def _in_proj(a, a_s, w, layer, col0, n_out, tm, tn, transposed=False):
    m, k = a.shape
    r = a_s.shape[0]
    assert m % tm == 0 and n_out % tn == 0
    w_spec, wb_shape, dims = _weight_spec(k, tn, layer, col0, transposed)
    return pl.pallas_call(
        functools.partial(_in_proj_kernel, dims=dims),
        grid=(n_out // tn, m // tm),
        in_specs=[pl.BlockSpec((tm, k), lambda j, i: (i, 0)),
                  pl.BlockSpec((r, k), lambda j, i: (0, 0)),
                  w_spec],
        out_specs=[pl.BlockSpec((tm, tn), lambda j, i: (i, j)),
                   pl.BlockSpec((r, tn), lambda j, i: (0, j))],
        out_shape=[jax.ShapeDtypeStruct((m, n_out), F32), jax.ShapeDtypeStruct((r, n_out), F32)],
        scratch_shapes=[pltpu.VMEM(wb_shape, BF16)],
        compiler_params=_params("arbitrary", "arbitrary"),
        name="in_proj",
    )(a, a_s, w)


def _post_norm(h, x, gt, g, b):
    r = DEEPNORM_ALPHA * x + (1.0 + gt) * h
    mu = jnp.mean(r, axis=-1, keepdims=True)
    rc = r - mu
    var = jnp.mean(rc * rc, axis=-1, keepdims=True)
    return rc * lax.rsqrt(var + LN_EPS) * g + b


def _out_even_kernel(ya_ref, yb_ref, w_ref, x_ref, gt_ref, g_ref, b_ref, sc_ref, sh_ref, xo_ref, uo_ref):
    ka = ya_ref.shape[1]
    h = _dot(ya_ref[...], w_ref[0:ka, :]) + _dot(yb_ref[...], w_ref[ka:, :])
    y = _post_norm(h, x_ref[...], gt_ref[0], g_ref[...], b_ref[...])
    xo_ref[...] = y
    uo_ref[...] = (y * (1.0 + sc_ref[0]) + sh_ref[0]).astype(uo_ref.dtype)


def _out_even(ya, yb, w, x, gt, g, b, sc, sh, rows_per_mod, tm):
    m, d = x.shape
    ka, kb = ya.shape[1], yb.shape[1]
    r = gt.shape[1]
    per = rows_per_mod // tm
    mod = pl.BlockSpec((1, r, d), lambda i: (i // per, 0, 0))
    vec = pl.BlockSpec((1, d), lambda i: (0, 0))
    return pl.pallas_call(
        _out_even_kernel,
        grid=(m // tm,),
        in_specs=[pl.BlockSpec((tm, ka), lambda i: (i, 0)),
                  pl.BlockSpec((tm, kb), lambda i: (i, 0)),
                  pl.BlockSpec((ka + kb, d), lambda i: (0, 0)),
                  pl.BlockSpec((tm, d), lambda i: (i, 0)),
                  mod, vec, vec, mod, mod],
        out_specs=[pl.BlockSpec((tm, d), lambda i: (i, 0)),
                   pl.BlockSpec((tm, d), lambda i: (i, 0))],
        out_shape=[jax.ShapeDtypeStruct((m, d), F32), jax.ShapeDtypeStruct((m, d), BF16)],
        compiler_params=_params("arbitrary"),
        name="out_proj_even",
    )(ya, yb, w, x, gt, g, b, sc, sh)


def _out_odd_kernel(a_ref, w_ref, x_ref, gt_ref, g_ref, b_ref, sc_ref, sh_ref, xo_ref, uo_ref, acc_ref):
    k = pl.program_id(1)

    @pl.when(k == 0)
    def _():
        acc_ref[...] = jnp.zeros_like(acc_ref)

    acc_ref[...] += _dot(a_ref[...], w_ref[...])

    @pl.when(k == pl.num_programs(1) - 1)
    def _():
        y = _post_norm(acc_ref[...], x_ref[...], gt_ref[0], g_ref[...], b_ref[...])
        xo_ref[...] = y
        uo_ref[...] = (y * (1.0 + sc_ref[0]) + sh_ref[0]).astype(uo_ref.dtype)


def _out_odd(a, w, x, gt, g, b, sc, sh, rows_per_mod, tm, tk):
    m, d = x.shape
    kdim = a.shape[1]
    r = gt.shape[1]
    per = rows_per_mod // tm
    mod = pl.BlockSpec((1, r, d), lambda i, k: (i // per, 0, 0))
    vec = pl.BlockSpec((1, d), lambda i, k: (0, 0))
    return pl.pallas_call(
        _out_odd_kernel,
        grid=(m // tm, kdim // tk),
        in_specs=[pl.BlockSpec((tm, tk), lambda i, k: (i, k)),
                  pl.BlockSpec((tk, d), lambda i, k: (k, 0)),
                  pl.BlockSpec((tm, d), lambda i, k: (i, 0)),
                  mod, vec, vec, mod, mod],
        out_specs=[pl.BlockSpec((tm, d), lambda i, k: (i, 0)),
                   pl.BlockSpec((tm, d), lambda i, k: (i, 0))],
        out_shape=[jax.ShapeDtypeStruct((m, d), F32), jax.ShapeDtypeStruct((m, d), BF16)],
        scratch_shapes=[pltpu.VMEM((tm, d), F32)],
        compiler_params=_params("arbitrary", "arbitrary"),
        name="out_proj_odd",
    )(a, w, x, gt, g, b, sc, sh)


CONV_A_HALO = 32
CONV_A_ROWS = 64


def _group_norm_gate(y, gate, g, b):
    outs = []
    for s in range(y.shape[1] // GROUP_A):
        sl = slice(s * GROUP_A, (s + 1) * GROUP_A)
        seg = y[:, sl]
        mu = jnp.mean(seg, axis=-1, keepdims=True)
        sc = seg - mu
        var = jnp.mean(sc * sc, axis=-1, keepdims=True)
        yn = sc * lax.rsqrt(var + LN_EPS) * g[:, sl] + b[:, sl]
        outs.append(_silu(yn) * _silu(gate[:, sl]))
    return outs[0] if len(outs) == 1 else jnp.concatenate(outs, axis=1)


def _conv_a_kernel(val_ref, glu_ref, gate_ref, w_ref, g_ref, b_ref, y_ref, st_ref, ext_ref):
    t = pl.program_id(2)
    tt = val_ref.shape[0]
    kw = w_ref.shape[0]
    off = CONV_A_HALO - (kw - 1)

    @pl.when(t == 0)
    def _():
        ext_ref[0:CONV_A_HALO, :] = jnp.zeros((CONV_A_HALO, ext_ref.shape[1]), F32)

    ext_ref[CONV_A_HALO:, :] = val_ref[...] * _sigmoid(glu_ref[...])
    sub = 8
    for r0 in range(0, tt, CONV_A_ROWS):
        acc = None
        for res in range(sub):
            part = None
            for j in range(kw):
                if (off + j) % sub == res:
                    a0 = r0 + off + j - res
                    term = w_ref[j:j + 1, :] * ext_ref[a0:a0 + CONV_A_ROWS + (sub if res else 0), :]
                    part = term if part is None else part + term
            if part is not None:
                part = part[res:res + CONV_A_ROWS, :]
                acc = part if acc is None else acc + part
        y = _group_norm_gate(acc, gate_ref[r0:r0 + CONV_A_ROWS, :], g_ref[...], b_ref[...])
        y_ref[r0:r0 + CONV_A_ROWS, :] = y.astype(y_ref.dtype)

    @pl.when(t == pl.num_programs(2) - 1)
    def _():
        st_ref[0] = ext_ref[tt + off:tt + CONV_A_HALO, :]

    ext_ref[0:CONV_A_HALO, :] = ext_ref[tt:tt + CONV_A_HALO, :]


def _conv_a_prompt(h, conv_w, gn_g, gn_b, batch, seq, tt=256, tc=256):
    nt = seq // tt
    nc = W_A // tc
    kw = conv_w.shape[0]
    col = lambda base: pl.BlockSpec((tt, tc), lambda b, c, t: (b * nt + t, base * nc + c))
    vec = pl.BlockSpec((1, tc), lambda b, c, t: (0, c))
    return pl.pallas_call(
        _conv_a_kernel,
        grid=(batch, nc, nt),
        in_specs=[col(0), col(1), col(2),
                  pl.BlockSpec((kw, tc), lambda b, c, t: (0, c)), vec, vec],
        out_specs=[pl.BlockSpec((tt, tc), lambda b, c, t: (b * nt + t, c)),
                   pl.BlockSpec((1, kw - 1, tc), lambda b, c, t: (b, 0, c))],
        out_shape=[jax.ShapeDtypeStruct((batch * seq, W_A), BF16),
                   jax.ShapeDtypeStruct((batch, kw - 1, W_A), F32)],
        scratch_shapes=[pltpu.VMEM((CONV_A_HALO + tt, tc), F32)],
        compiler_params=_params("arbitrary", "arbitrary", "arbitrary"),
        name="conv_a_prompt",
    )(h, h, h, conv_w, gn_g.reshape(1, -1), gn_b.reshape(1, -1))


def _conv_a_sample_kernel(h_ref, st_ref, w_ref, g_ref, b_ref, y_ref, so_ref):
    nseq, kw1, _ = st_ref.shape
    glu = h_ref[:, 0:W_A] * _sigmoid(h_ref[:, W_A:2 * W_A])
    rows = []
    for s in range(nseq):
        conv = jnp.sum(st_ref[s] * w_ref[0:kw1, :], axis=0, keepdims=True) + glu[s:s + 1, :] * w_ref[kw1:kw1 + 1, :]
        rows.append(conv)
        so_ref[s, 0:kw1 - 1, :] = st_ref[s, 1:kw1, :]
        so_ref[s, kw1 - 1:kw1, :] = glu[s:s + 1, :]
    rows.append(jnp.zeros((h_ref.shape[0] - nseq, W_A), F32))
    conv = jnp.concatenate(rows, axis=0)
    y = _group_norm_gate(conv, h_ref[:, 2 * W_A:3 * W_A], g_ref[...], b_ref[...])
    y_ref[...] = y.astype(y_ref.dtype)


def _conv_a_sample(h_s, state, conv_w, gn_g, gn_b):
    rows = h_s.shape[0]
    return pl.pallas_call(
        _conv_a_sample_kernel,
        out_shape=[jax.ShapeDtypeStruct((rows, W_A), BF16), jax.ShapeDtypeStruct(state.shape, F32)],
        compiler_params=pltpu.CompilerParams(vmem_limit_bytes=VMEM_LIMIT),
        name="conv_a_sample",
    )(h_s, state, conv_w, gn_g.reshape(1, -1), gn_b.reshape(1, -1))


LOG2_E = 1.4426950408889634
ATTN_BLOCKS_PER_ITER = 8

def _attn_prompt_kernel(bias_ref, q_ref, k_ref, v_ref, gate_ref, o_ref, acc_ref, later_ref):
    h = pl.program_id(1)
    i = pl.program_id(2)
    blk = q_ref.shape[0]
    bias = bias_ref[h] * LOG2_E
    q = (q_ref[...] * (SB_HEAD_DIM ** -0.5 * LOG2_E)).astype(BF16)
    row = lax.broadcasted_iota(jnp.int32, (blk, blk), 0)
    col = lax.broadcasted_iota(jnp.int32, (blk, blk), 1)
    after = jnp.where(row > col, 1.0, 0.0).astype(BF16)
    visible = col < row

    acc_ref[...] = jnp.zeros_like(acc_ref)
    later_ref[...] = jnp.zeros_like(later_ref)

    def blocks(first, count, diag_first):
        js = [first - t for t in range(count)]
        masked = [diag_first and t == 0 for t in range(count)]
        zs, sps = [], []
        for j, m in zip(js, masked):
            start = pl.multiple_of(j * blk, blk)
            z = _dot(q, k_ref[pl.ds(start, blk), :].astype(BF16), NT) + bias
            sp = jnp.maximum(z, 0.0) + jnp.log2(1.0 + jnp.exp2(-jnp.abs(z)))
            if m:
                sp = jnp.where(visible, sp, 0.0)
            zs.append(z)
            sps.append(sp)
        later = later_ref[...]
        out = None
        for j, m, z, sp in zip(js, masked, zs, sps):
            start = pl.multiple_of(j * blk, blk)
            inside = _dot(sp.astype(BF16), after)
            w = jnp.exp2(z - sp - inside - later)
            if m:
                w = jnp.where(visible, w, 0.0)
            pv = _dot(w.astype(BF16), v_ref[pl.ds(start, blk), :].astype(BF16))
            out = pv if out is None else out + pv
            later = later + jnp.sum(sp, axis=-1, keepdims=True)
        acc_ref[...] += out
        later_ref[...] = later

    nb = ATTN_BLOCKS_PER_ITER
    total = i + 1
    lead = total % nb
    size = nb // 2
    while size >= 1:
        done = (lead // (2 * size)) * (2 * size)
        take = (lead // size) % 2 == 1

        @pl.when(jnp.logical_and(take, done == 0))
        def _(size=size):
            blocks(i, size, True)

        @pl.when(jnp.logical_and(take, done != 0))
        def _(size=size, done=done):
            blocks(i - done, size, False)

        size //= 2

    @pl.when(lead == 0)
    def _():
        blocks(i, nb, True)

    before = jnp.where(lead == 0, nb, lead)

    def body(g, carry):
        blocks(i - before - nb * g, nb, False)
        return carry

    lax.fori_loop(0, (total - before) // nb, body, 0)

    o_ref[...] = (acc_ref[...] * _silu(gate_ref[...])).astype(o_ref.dtype)


def _attn_prompt(h, sb_bias, batch, seq, blk=256):
    nq = seq // blk
    base = 3 * W_A // SB_HEAD_DIM
    return pl.pallas_call(
        _attn_prompt_kernel,
        grid_spec=pltpu.PrefetchScalarGridSpec(
            num_scalar_prefetch=0,
            grid=(batch, SB_HEADS, nq),
            in_specs=[pl.BlockSpec(memory_space=pltpu.SMEM),
                      pl.BlockSpec((blk, SB_HEAD_DIM), lambda b, hh, i: (b * nq + i, base + hh)),
                      pl.BlockSpec((seq, SB_HEAD_DIM), lambda b, hh, i: (b, base + SB_HEADS + hh)),
                      pl.BlockSpec((seq, SB_HEAD_DIM), lambda b, hh, i: (b, base + 2 * SB_HEADS + hh)),
                      pl.BlockSpec((blk, SB_HEAD_DIM), lambda b, hh, i: (b * nq + i, base + 3 * SB_HEADS + hh))],
            out_specs=pl.BlockSpec((blk, SB_HEAD_DIM), lambda b, hh, i: (b * nq + i, hh)),
            scratch_shapes=[pltpu.VMEM((blk, SB_HEAD_DIM), F32), pltpu.VMEM((blk, 1), F32)]),
        out_shape=jax.ShapeDtypeStruct((batch * seq, W_B), BF16),
        compiler_params=_params("arbitrary", "arbitrary", "arbitrary"),
        name="attn_prompt",
    )(sb_bias, h, h, h, h)


PAGES_PER_STEP = 8


def _attn_sample_kernel(pt_ref, q_ref, bias_ref, gate_ref, own_ref, tok_ref, tokt_ref, *refs):
    npg = PAGES_PER_STEP
    k_refs, v_refs = refs[:npg], refs[npg:2 * npg]
    o_ref, acc_ref, later_ref = refs[2 * npg:]
    p = pl.program_id(1)
    rows = q_ref.shape[1]
    page = k_refs[0].shape[2] // SB_HEADS

    @pl.when(p == 0)
    def _():
        acc_ref[...] = jnp.zeros_like(acc_ref)
        later_ref[...] = jnp.zeros_like(later_ref)

    q = (q_ref[0] * (SB_HEAD_DIM ** -0.5)).astype(BF16)
    own = own_ref[...]
    r2 = lax.broadcasted_iota(jnp.int32, (page, page), 0)
    c2 = lax.broadcasted_iota(jnp.int32, (page, page), 1)
    after = jnp.where(r2 > c2, 1.0, 0.0).astype(BF16)

    def two_pass(x, m):
        hi, lo = _split(x)
        both = _dot(jnp.concatenate([hi, lo], axis=0), m)
        return both[0:rows, :] + both[rows:, :]

    pages = range(npg)
    zs = [two_pass(_dot(q, k_refs[r][0, 0].astype(BF16), NT) * own, tok_ref[...]) + bias_ref[...] for r in pages]
    sps = [_softplus(z) for z in zs]
    inside = [_dot(sp.astype(BF16), after) for sp in sps]
    later = later_ref[...]
    ws = []
    for r in pages:
        ws.append(jnp.exp(zs[r] - sps[r] - inside[r] - later).astype(BF16))
        later = later + jnp.sum(sps[r], axis=-1, keepdims=True)
    spread = [(_dot(w, tokt_ref[...]) * own).astype(BF16) for w in ws]
    out = acc_ref[...]
    for r in pages:
        out = out + _dot(spread[r], v_refs[r][0, 0].astype(BF16))
    acc_ref[...] = out
    later_ref[...] = later

    @pl.when(p == pl.num_programs(1) - 1)
    def _():
        o_ref[0] = acc_ref[...] * _silu(gate_ref[0])


def _attn_sample(q, gate, bias, cache_k, cache_v, page_table, layer):
    nseq, rows, _ = q.shape
    n_pages = page_table.shape[1]
    npg = PAGES_PER_STEP
    blk = cache_k.shape[2]

    def page_spec(r):
        return pl.BlockSpec((1, 1, blk, SB_HEAD_DIM),
                            lambda s, p, pt: (layer, pt[s, n_pages - 1 - (p * npg + r)], 0, 0))

    row_spec = pl.BlockSpec((1, rows, SB_HEAD_DIM), lambda s, p, pt: (s, 0, 0))
    page = blk // SB_HEADS
    r_head = jnp.arange(blk, dtype=jnp.int32) % SB_HEADS
    r_tok = jnp.arange(blk, dtype=jnp.int32) // SB_HEADS
    own = (jnp.arange(rows, dtype=jnp.int32)[:, None] == r_head[None, :]).astype(F32)
    tok = (r_tok[:, None] == jnp.arange(page, dtype=jnp.int32)[None, :]).astype(BF16)
    whole = lambda shape: pl.BlockSpec(shape, lambda s, p, pt: (0, 0))
    return pl.pallas_call(
        _attn_sample_kernel,
        grid_spec=pltpu.PrefetchScalarGridSpec(
            num_scalar_prefetch=1,
            grid=(nseq, n_pages // npg),
            in_specs=[row_spec, whole((rows, 1)), row_spec, whole((rows, blk)), whole((blk, page)), whole((page, blk))]
                     + [page_spec(r) for r in range(npg)] + [page_spec(r) for r in range(npg)],
            out_specs=row_spec,
            scratch_shapes=[pltpu.VMEM((rows, SB_HEAD_DIM), F32), pltpu.VMEM((rows, 1), F32)]),
        out_shape=jax.ShapeDtypeStruct((nseq, rows, SB_HEAD_DIM), F32),
        compiler_params=_params("arbitrary", "arbitrary"),
        name="attn_sample",
    )(page_table, q, bias, gate, own, tok, tok.T, *([cache_k] * npg), *([cache_v] * npg))


def _gdn_gate_kernel(u_ref, wt_ref, alog_ref, dt_ref, beta_ref, gc_ref):
    raw = _dot(wt_ref[...].astype(BF16), u_ref[...], NT)
    hv = raw.shape[0] // 2
    beta_ref[0] = _sigmoid(raw[0:hv, :])
    g = -jnp.exp(alog_ref[...]) * _softplus(raw[hv:, :] + dt_ref[...])
    r2 = lax.broadcasted_iota(jnp.int32, (GDN_CHUNK, GDN_CHUNK), 0)
    c2 = lax.broadcasted_iota(jnp.int32, (GDN_CHUNK, GDN_CHUNK), 1)
    upto = jnp.where(r2 <= c2, 1.0, 0.0).astype(BF16)
    gh, gl = _split(g)
    gl2 = (g - gh.astype(F32) - gl.astype(F32)).astype(BF16)
    for c in range(g.shape[1] // GDN_CHUNK):
        sl = slice(c * GDN_CHUNK, (c + 1) * GDN_CHUNK)
        gc_ref[0, :, sl] = _dot(gh[:, sl], upto) + (_dot(gl[:, sl], upto) + _dot(gl2[:, sl], upto))


def _gdn_gates(u, w_ba_t, a_log, dt_bias, batch, seq, tt=512):
    nt = seq // tt
    hv2, d = w_ba_t.shape
    hv = hv2 // 2
    out = pl.BlockSpec((1, hv, tt), lambda b, t: (b, 0, t))
    colv = pl.BlockSpec((hv, 1), lambda b, t: (0, 0))
    return pl.pallas_call(
        _gdn_gate_kernel,
        grid=(batch, nt),
        in_specs=[pl.BlockSpec((tt, d), lambda b, t: (b * nt + t, 0)),
                  pl.BlockSpec((hv2, d), lambda b, t: (0, 0)), colv, colv],
        out_specs=[out, out],
        out_shape=[jax.ShapeDtypeStruct((batch, hv, seq), F32)] * 2,
        compiler_params=_params("arbitrary", "arbitrary"),
        name="gdn_gates",
    )(u, w_ba_t, a_log.reshape(hv, 1), dt_bias.reshape(hv, 1))


CONV_C_HALO = 8


def _qk_norm(y, first):
    is_q = first < GDN_QK_W
    is_qk = first < 2 * GDN_QK_W
    outs = []
    for s in range(y.shape[1] // GDN_HEAD_DIM):
        seg = y[:, s * GDN_HEAD_DIM:(s + 1) * GDN_HEAD_DIM]
        rs = lax.rsqrt(jnp.sum(seg * seg, axis=-1, keepdims=True) + RMS_EPS)
        f = jnp.where(is_q, rs * (GDN_HEAD_DIM ** -0.5), jnp.where(is_qk, rs, 1.0))
        outs.append(seg * f)
    return outs[0] if len(outs) == 1 else jnp.concatenate(outs, axis=1)


CONV_C_ROWS = 128


def _in_proj_conv_kernel(a_ref, as_ref, w_ref, cw_ref, y_ref, os_ref, st_ref, wb_ref, ext_ref, *, dims, tiles_per_seq):
    j = pl.program_id(0)
    i = pl.program_id(1)
    tm, tn = y_ref.shape
    kw = cw_ref.shape[0]
    off = CONV_C_HALO - (kw - 1)

    @pl.when(i == 0)
    def _():
        wb_ref[...] = w_ref[0].astype(BF16)
        os_ref[...] = _dot(as_ref[...], wb_ref[...], dims)

    @pl.when(i % tiles_per_seq == 0)
    def _():
        ext_ref[0:CONV_C_HALO, :] = jnp.zeros((CONV_C_HALO, tn), F32)

    @pl.when(i % tiles_per_seq != 0)
    def _():
        ext_ref[0:CONV_C_HALO, :] = ext_ref[tm:tm + CONV_C_HALO, :]

    ext_ref[CONV_C_HALO:, :] = _dot(a_ref[...], wb_ref[...], dims)
    st_ref[0] = ext_ref[tm + off:tm + CONV_C_HALO, :]
    for r0 in range(0, tm, CONV_C_ROWS):
        acc = cw_ref[0:1, :] * ext_ref[r0 + off:r0 + off + CONV_C_ROWS, :]
        for t in range(1, kw):
            acc = acc + cw_ref[t:t + 1, :] * ext_ref[r0 + off + t:r0 + off + t + CONV_C_ROWS, :]
        y_ref[r0:r0 + CONV_C_ROWS, :] = _qk_norm(_silu(acc), j * tn)


def _in_proj_conv(a, a_s, w, layer, n_out, conv_w, batch, seq, tm, tn, transposed):
    m, k = a.shape
    r = a_s.shape[0]
    kw = conv_w.shape[0]
    assert m % tm == 0 and n_out % tn == 0 and seq % tm == 0 and tm % CONV_C_ROWS == 0
    tiles_per_seq = seq // tm
    w_spec, wb_shape, dims = _weight_spec(k, tn, layer, 0, transposed)
    return pl.pallas_call(
        functools.partial(_in_proj_conv_kernel, dims=dims, tiles_per_seq=tiles_per_seq),
        grid=(n_out // tn, m // tm),
        in_specs=[pl.BlockSpec((tm, k), lambda j, i: (i, 0)),
                  pl.BlockSpec((r, k), lambda j, i: (0, 0)),
                  w_spec,
                  pl.BlockSpec((kw, tn), lambda j, i: (0, j))],
        out_specs=[pl.BlockSpec((tm, tn), lambda j, i: (i, j)),
                   pl.BlockSpec((r, tn), lambda j, i: (0, j)),
                   pl.BlockSpec((1, kw - 1, tn), lambda j, i: (i // tiles_per_seq, 0, j))],
        out_shape=[jax.ShapeDtypeStruct((m, n_out), F32), jax.ShapeDtypeStruct((r, n_out), F32),
                   jax.ShapeDtypeStruct((batch, kw - 1, n_out), F32)],
        scratch_shapes=[pltpu.VMEM(wb_shape, BF16), pltpu.VMEM((CONV_C_HALO + tm, tn), F32)],
        compiler_params=_params("arbitrary", "arbitrary"),
        name="in_proj_conv",
    )(a, a_s, w, conv_w)


def _gdn_prompt_kernel(q_ref, k_ref, v_ref, z_ref, beta_ref, gc_ref, gw_ref, o_ref, s_out_ref, s_ref):
    t = pl.program_id(2)
    c = GDN_CHUNK
    tt = q_ref.shape[0]
    rep = v_ref.shape[1] // GDN_HEAD_DIM

    @pl.when(t == 0)
    def _():
        s_ref[...] = jnp.zeros_like(s_ref)

    row = lax.broadcasted_iota(jnp.int32, (c, c), 0)
    col = lax.broadcasted_iota(jnp.int32, (c, c), 1)
    incl = row >= col
    strict = row > col
    eye = jnp.where(row == col, 1.0, 0.0)
    merge = []
    m = 1
    while m < c:
        bi, bj = row // m, col // m
        merge.append(jnp.logical_and(bi - bj == 1, bj % 2 == 0))
        m *= 2

    bf = lambda x: x.astype(BF16)
    nchunk = tt // c
    units = [(n, hv) for n in range(nchunk) for hv in range(rep)]
    rows_of = lambda n: slice(n * c, (n + 1) * c)
    cols_of = lambda hv: slice(hv * GDN_HEAD_DIM, (hv + 1) * GDN_HEAD_DIM)

    kq, kbf = {}, {}
    for n in range(nchunk):
        kbf[n] = bf(k_ref[rows_of(n), :])
        kq[n] = _dot(jnp.concatenate([kbf[n], bf(q_ref[rows_of(n), :])], axis=0), kbf[n], NT)
    spread = lambda r: jnp.broadcast_to(r, (c, c))
    b_row, be_row, b_col, be_col, e_col, kd_col, e_last, lneg, intra, xo = {}, {}, {}, {}, {}, {}, {}, {}, {}, {}
    for un in units:
        n, hv = un
        gr = gc_ref[0, hv, :, rows_of(n)]
        br = beta_ref[0, hv, :, rows_of(n)]
        g_last = gr[:, c - 1:c]
        er = jnp.exp(gr)
        g_row = spread(gr)
        b_row[un] = spread(br)
        be_row[un] = spread(br * er)
        b_col[un] = b_row[un].T
        be_col[un] = be_row[un].T
        e_col[un] = spread(er).T
        kd_col[un] = spread(jnp.exp(g_last - gr)).T
        e_last[un] = jnp.exp(g_last)
        decay = jnp.exp(g_row.T - g_row)
        lneg[un] = jnp.where(strict, (-kq[n][0:c, :]) * b_col[un] * decay, 0.0)
        intra[un] = bf(jnp.where(incl, kq[n][c:, :] * decay, 0.0))
        xo[un] = jnp.where(merge[0], lneg[un], 0.0)
    for lvl in merge[1:]:
        for un in units:
            cn = jnp.where(lvl, lneg[un], 0.0)
            y = cn + _dot(bf(xo[un]), bf(cn))
            xo[un] = xo[un] + (y + _dot(bf(y), bf(xo[un])))
    u_val, wq, k_dec_t = {}, {}, {}
    for un in units:
        n, hv = un
        k = k_ref[rows_of(n), :]
        v = v_ref[rows_of(n), cols_of(hv)]
        u_val[un] = v * b_col[un] + _dot(bf(xo[un] * b_row[un]), bf(v))
        w_key = k * be_col[un] + _dot(bf(xo[un] * be_row[un]), kbf[n])
        wq[un] = bf(jnp.concatenate([w_key, q_ref[rows_of(n), :] * e_col[un]], axis=0))
        k_dec_t[un] = bf((k * kd_col[un]).T)

    for un in units:
        n, hv = un
        s = s_ref[hv]
        ws = _dot(wq[un], bf(s))
        v_new = bf(u_val[un] - ws[0:c, :])
        o = ws[c:, :] + _dot(intra[un], v_new)
        s_ref[hv] = s * e_last[un] + _dot(k_dec_t[un], v_new)
        o = o * lax.rsqrt(jnp.mean(o * o, axis=-1, keepdims=True) + RMS_EPS) * gw_ref[...]
        o_ref[rows_of(n), cols_of(hv)] = (o * _silu(z_ref[rows_of(n), cols_of(hv)])).astype(o_ref.dtype)

    @pl.when(t == pl.num_programs(2) - 1)
    def _():
        s_out_ref[0] = s_ref[...]


def _gdn_prompt(qkv, h, z_col0, beta, gc, gnorm_w, batch, seq, tt=1024):
    tt = min(tt, seq)
    assert seq % tt == 0 and tt % GDN_CHUNK == 0
    nt = seq // tt
    rep = GDN_V_HEADS // GDN_QK_HEADS
    dh = GDN_HEAD_DIM
    vw = rep * dh
    gate = pl.BlockSpec((1, rep, 1, tt), lambda b, hq, t: (b, hq, 0, t))
    return pl.pallas_call(
        _gdn_prompt_kernel,
        grid=(batch, GDN_QK_HEADS, nt),
        in_specs=[pl.BlockSpec((tt, dh), lambda b, hq, t: (b * nt + t, hq)),
                  pl.BlockSpec((tt, dh), lambda b, hq, t: (b * nt + t, GDN_QK_HEADS + hq)),
                  pl.BlockSpec((tt, vw), lambda b, hq, t: (b * nt + t, 2 * GDN_QK_W // vw + hq)),
                  pl.BlockSpec((tt, vw), lambda b, hq, t: (b * nt + t, z_col0 // vw + hq)),
                  gate, gate,
                  pl.BlockSpec((1, dh), lambda b, hq, t: (0, 0))],
        out_specs=[pl.BlockSpec((tt, vw), lambda b, hq, t: (b * nt + t, hq)),
                   pl.BlockSpec((1, rep, dh, dh), lambda b, hq, t: (b, hq, 0, 0))],
        out_shape=[jax.ShapeDtypeStruct((batch * seq, GDN_V_W), BF16),
                   jax.ShapeDtypeStruct((batch, GDN_V_HEADS, dh, dh), F32)],
        scratch_shapes=[pltpu.VMEM((rep, dh, dh), F32)],
        compiler_params=_params("arbitrary", "arbitrary", "arbitrary"),
        name="gdn_prompt",
    )(qkv, qkv, qkv, h, beta, gc, gnorm_w.reshape(1, dh))


def _conv_c_sample_kernel(h_ref, st_ref, w_ref, u_ref, wba_ref, alog_ref, dt_ref, y_ref, so_ref, beta_ref, eg_ref):
    nseq, kw1, width = st_ref.shape
    tc = 1024
    for c0 in range(0, width, tc):
        cs = slice(c0, c0 + tc)
        rows = []
        for s in range(nseq):
            x_new = h_ref[s:s + 1, cs]
            conv = jnp.sum(st_ref[s, :, cs] * w_ref[0:kw1, cs], axis=0, keepdims=True) + x_new * w_ref[kw1:kw1 + 1, cs]
            rows.append(conv)
            so_ref[s, 0:kw1 - 1, cs] = st_ref[s, 1:kw1, cs]
            so_ref[s, kw1 - 1:kw1, cs] = x_new
        rows.append(jnp.zeros((h_ref.shape[0] - nseq, tc), F32))
        y_ref[:, cs] = _qk_norm(_silu(jnp.concatenate(rows, axis=0)), c0)
    raw = _dot(u_ref[...], wba_ref[...].astype(BF16), NT)
    hv = raw.shape[1] // 2
    beta_ref[...] = _sigmoid(raw[:, 0:hv])
    eg_ref[...] = jnp.exp(-jnp.exp(alog_ref[...]) * _softplus(raw[:, hv:] + dt_ref[...]))


def _conv_c_sample(h_s, state, conv_w, u_s, w_ba, a_log, dt_bias):
    rows = h_s.shape[0]
    hv = a_log.shape[0]
    return pl.pallas_call(
        _conv_c_sample_kernel,
        out_shape=[jax.ShapeDtypeStruct((rows, GDN_CONV_DIM), F32), jax.ShapeDtypeStruct(state.shape, F32),
                   jax.ShapeDtypeStruct((rows, hv), F32), jax.ShapeDtypeStruct((rows, hv), F32)],
        compiler_params=pltpu.CompilerParams(vmem_limit_bytes=VMEM_LIMIT),
        name="conv_c_sample",
    )(h_s, state, conv_w, u_s, w_ba, a_log.reshape(1, hv), dt_bias.reshape(1, hv))


def _gdn_sample_kernel(beta_ref, eg_ref, qc_ref, kc_ref, v_ref, z_ref, gw_ref, s_ref, o_ref, so_ref):
    sq = pl.program_id(0)
    rep = GDN_V_HEADS // GDN_QK_HEADS
    for hv in range(GDN_V_HEADS):
        hq = hv // rep
        k_col = kc_ref[0, :, hq:hq + 1]
        q_col = qc_ref[0, :, hq:hq + 1]
        s = s_ref[0, hv] * eg_ref[sq, hv]
        delta = (v_ref[0, hv:hv + 1, :] - jnp.sum(s * k_col, axis=0, keepdims=True)) * beta_ref[sq, hv]
        s = s + k_col * delta
        so_ref[0, hv] = s
        o = jnp.sum(s * q_col, axis=0, keepdims=True)
        o = o * lax.rsqrt(jnp.mean(o * o, axis=-1, keepdims=True) + RMS_EPS) * gw_ref[...]
        o_ref[0, hv:hv + 1, :] = o * _silu(z_ref[0, hv:hv + 1, :])


def _gdn_sample(beta, eg, q_col, k_col, v, z, gnorm_w, state):
    nseq, hv, dk, dv = state.shape
    hq = q_col.shape[2]
    smem = pl.BlockSpec(memory_space=pltpu.SMEM)
    colspec = pl.BlockSpec((1, dk, hq), lambda s: (s, 0, 0))
    rowspec = pl.BlockSpec((1, hv, dv), lambda s: (s, 0, 0))
    stspec = pl.BlockSpec((1, hv, dk, dv), lambda s: (s, 0, 0, 0))
    return pl.pallas_call(
        _gdn_sample_kernel,
        grid=(nseq,),
        in_specs=[smem, smem, colspec, colspec, rowspec, rowspec,
                  pl.BlockSpec((1, dv), lambda s: (0, 0)), stspec],
        out_specs=[rowspec, stspec],
        out_shape=[jax.ShapeDtypeStruct((nseq, hv, dv), F32), jax.ShapeDtypeStruct(state.shape, F32)],
        compiler_params=_params("arbitrary"),
        name="gdn_sample",
    )(beta, eg, q_col, k_col, v, z, gnorm_w.reshape(1, dv), state)


def kernel(x_prompt, x_sample, c_prompt, c_sample, cache_k, cache_v, page_table, state_conv_a, state_conv_c, state_delta, w_ada, b_ada, ln_g, ln_b, w_in_even, conv_w_a, gn_g_a, gn_b_a, sb_bias, w_out_even, w_in_odd, conv_w_c, a_log_c, dt_bias_c, gnorm_w_c, w_out_odd):
    batch, seq, d = x_prompt.shape
    nseq = x_sample.shape[0]
    n_even, n_pool, page_size = cache_k.shape[:3]
    np_rows = batch * seq
    pad = SAMPLE_ROWS - nseq

    c_all = jnp.concatenate([c_prompt, c_sample, jnp.zeros((SAMPLE_ROWS - batch - nseq, d), F32)], axis=0)
    mod = _ada_all(c_all, w_ada, b_ada)

    def mods(layer):
        m = mod[layer]
        parts = [m[:, i * d:(i + 1) * d] for i in range(3)]
        pm = [p[0:batch].reshape(batch, 1, d) for p in parts]
        sm = [jnp.pad(p[batch:batch + nseq], ((0, pad), (0, 0))).reshape(1, SAMPLE_ROWS, d) for p in parts]
        return pm, sm

    xp = x_prompt.reshape(np_rows, d)
    xs = jnp.pad(x_sample.reshape(nseq, d), ((0, pad), (0, 0)))
    (sh_p, sc_p, _), (sh_s, sc_s, _) = mods(0)
    up = _modulate(xp, sc_p, sh_p, seq, 512)
    us = _modulate(xs, sc_s, sh_s, SAMPLE_ROWS, SAMPLE_ROWS)

    w_in_odd_t = jnp.swapaxes(w_in_odd, 1, 2)
    ck = cache_k.reshape(n_even, n_pool, page_size * SB_HEADS, SB_HEAD_DIM)
    cv = cache_v.reshape(n_even, n_pool, page_size * SB_HEADS, SB_HEAD_DIM)
    bias_pad = lambda b: jnp.pad(b, (0, SAMPLE_ROWS - SB_HEADS)).reshape(SAMPLE_ROWS, 1)
    heads_pad = lambda a: jnp.pad(a.reshape(nseq, SB_HEADS, SB_HEAD_DIM), ((0, 0), (0, SAMPLE_ROWS - SB_HEADS), (0, 0)))

    kp_l, vp_l, ks_l, vs_l = [], [], [], []
    cap_l, cas_l, ccp_l, ccs_l, dp_l, ds_l = [], [], [], [], [], []
    for layer in range(DEPTH):
        j = layer // 2
        (_, _, gt_p), (_, _, gt_s) = mods(layer)
        if layer + 1 < DEPTH:
            (shn_p, scn_p, _), (shn_s, scn_s, _) = mods(layer + 1)
        else:
            shn_p = scn_p = jnp.zeros((batch, 1, d), F32)
            shn_s = scn_s = jnp.zeros((1, SAMPLE_ROWS, d), F32)
        g_ln = ln_g[layer].reshape(1, d)
        b_ln = ln_b[layer].reshape(1, d)
        if layer % 2 == 0:
            w_out = w_out_even[j].astype(BF16)
            hp, hs = _in_proj(up, us, w_in_even, j, 0, w_in_even.shape[2], min(1024, np_rows), 1024)
            ya_p, buf_p = _conv_a_prompt(hp, conv_w_a[j], gn_g_a[j], gn_b_a[j], batch, seq)
            yb_p = _attn_prompt(hp, sb_bias[j], batch, seq)
            xp, up = _out_even(ya_p, yb_p, w_out, xp, gt_p, g_ln, b_ln, scn_p, shn_p, seq, 256)
            kp_l.append(hp[:, 3 * W_A + W_B:3 * W_A + 2 * W_B].reshape(batch, seq, SB_HEADS, SB_HEAD_DIM))
            vp_l.append(hp[:, 3 * W_A + 2 * W_B:3 * W_A + 3 * W_B].reshape(batch, seq, SB_HEADS, SB_HEAD_DIM))
            cap_l.append(buf_p)
            ya_s, buf_s = _conv_a_sample(hs, state_conv_a[j], conv_w_a[j], gn_g_a[j], gn_b_a[j])
            q_s = heads_pad(hs[:nseq, 3 * W_A:3 * W_A + W_B])
            gate_s = heads_pad(hs[:nseq, 3 * W_A + 3 * W_B:3 * W_A + 4 * W_B])
            o_s = _attn_sample(q_s, gate_s, bias_pad(sb_bias[j]), ck, cv, page_table, j)
            yb_s = jnp.pad(o_s[:, :SB_HEADS].reshape(nseq, W_B), ((0, pad), (0, 0))).astype(BF16)
            xs, us = _out_even(ya_s, yb_s, w_out, xs, gt_s, g_ln, b_ln, scn_s, shn_s, SAMPLE_ROWS, SAMPLE_ROWS)
            ks_l.append(hs[:nseq, 3 * W_A + W_B:3 * W_A + 2 * W_B].reshape(nseq, 1, SB_HEADS, SB_HEAD_DIM))
            vs_l.append(hs[:nseq, 3 * W_A + 2 * W_B:3 * W_A + 3 * W_B].reshape(nseq, 1, SB_HEADS, SB_HEAD_DIM))
            cas_l.append(buf_s)
        else:
            w_out = w_out_odd[j].astype(BF16)
            n_main = GDN_CONV_DIM + GDN_V_W
            w_ba_t = w_in_odd_t[j, n_main:, :]
            tm = min(1024, seq)
            qkv_p, qkv_s_raw, buf_p = _in_proj_conv(up, us, w_in_odd_t, j, GDN_CONV_DIM, conv_w_c[j], batch, seq,
                                                    tm, 1024, True)
            z_p, z_s_raw = _in_proj(up, us, w_in_odd_t, j, GDN_CONV_DIM // 1024, GDN_V_W, tm, 1024, True)
            hs = jnp.concatenate([qkv_s_raw, z_s_raw], axis=1)
            beta_p, gc_p = _gdn_gates(up, w_ba_t, a_log_c[j], dt_bias_c[j], batch, seq)
            o_p, s_p = _gdn_prompt(qkv_p, z_p, 0, beta_p.reshape(batch, GDN_V_HEADS, 1, seq),
                                   gc_p.reshape(batch, GDN_V_HEADS, 1, seq), gnorm_w_c[j], batch, seq)
            xp, up = _out_odd(o_p, w_out, xp, gt_p, g_ln, b_ln, scn_p, shn_p, seq, 512, 1024)
            ccp_l.append(buf_p)
            dp_l.append(s_p)
            qkv_s, buf_s, beta_s, eg_s = _conv_c_sample(hs, state_conv_c[j], conv_w_c[j], us, w_ba_t,
                                                        a_log_c[j], dt_bias_c[j])
            heads_col = lambda a: jnp.swapaxes(a.reshape(nseq, GDN_QK_HEADS, GDN_HEAD_DIM), 1, 2)
            q_col = heads_col(qkv_s[:nseq, 0:GDN_QK_W])
            k_col = heads_col(qkv_s[:nseq, GDN_QK_W:2 * GDN_QK_W])
            v_s = qkv_s[:nseq, 2 * GDN_QK_W:].reshape(nseq, GDN_V_HEADS, GDN_HEAD_DIM)
            z_s = hs[:nseq, GDN_CONV_DIM:n_main].reshape(nseq, GDN_V_HEADS, GDN_HEAD_DIM)
            o_s, s_s = _gdn_sample(beta_s[:nseq], eg_s[:nseq], q_col, k_col, v_s, z_s, gnorm_w_c[j], state_delta[j])
            a_s = jnp.pad(o_s.reshape(nseq, GDN_V_W), ((0, pad), (0, 0))).astype(BF16)
            xs, us = _out_odd(a_s, w_out, xs, gt_s, g_ln, b_ln, scn_s, shn_s, SAMPLE_ROWS, SAMPLE_ROWS, 1024)
            ccs_l.append(buf_s)
            ds_l.append(s_s)
    return (xp.reshape(batch, seq, d), xs[:nseq].reshape(nseq, 1, d),
            jnp.stack(kp_l), jnp.stack(vp_l), jnp.stack(ks_l), jnp.stack(vs_l),
            jnp.stack(cap_l), jnp.stack(cas_l), jnp.stack(ccp_l), jnp.stack(ccs_l),
            jnp.stack(dp_l), jnp.stack(ds_l))
```
